```python
import math
import jax, jax.numpy as jnp
from jax import lax
import numpy as np

D_MODEL = 2048
BATCH = 4
SEQ = 2048
DEPTH = 1
DEC_BATCH = 8
DEC_SEQ = 8
PAST_LEN = 16384
PAGE_SIZE = 128

D_MIX = D_MODEL
D_A = D_MIX // 2
D_B = D_MIX - D_A
HEAD_DIM_A = 128
N_HEADS_A = D_A // HEAD_DIM_A
HEAD_DIM_B = 128
N_HEADS_B = D_B // HEAD_DIM_B
D_IN = 4 * D_A + 3 * D_B
SPLITS = [D_A, 2 * D_A, 3 * D_A, 4 * D_A, 4 * D_A + D_B, 4 * D_A + 2 * D_B]
D_FF = 5632
GLA_CHUNK = 32
Q_BLOCK = 128
N_MOD = 9
SB_BIAS_INIT = -9.0
EPS = 1e-6

kernel_name = "hymba_hgrn2_stickbreak_macaron_adaln_step"


def rmsnorm(x, g):
    xf = x.astype(jnp.float32)
    y = xf * lax.rsqrt(jnp.mean(xf * xf, axis=-1, keepdims=True) + EPS)
    return (y * g.astype(jnp.float32)).astype(x.dtype)


def head_rmsnorm(o, g):
    of = o.astype(jnp.float32)
    y = of * lax.rsqrt(jnp.mean(of * of, axis=-1, keepdims=True) + EPS)
    return y.reshape(o.shape[0], o.shape[1], -1) * g.astype(jnp.float32)


def modulate(h, shift, scale):
    return h * (1.0 + scale[:, None, :]) + shift[:, None, :]


def swiglu(h, w_gate, w_up, w_down):
    return (jax.nn.silu(h @ w_gate) * (h @ w_up)) @ w_down


def forget_lower_bounds(lb_logits):
    logits = jnp.concatenate([lb_logits.astype(jnp.float32), jnp.zeros((1, lb_logits.shape[1]), jnp.float32)], axis=0)
    return jnp.cumsum(jax.nn.softmax(logits, axis=0), axis=0)[:lb_logits.shape[0]]


def gated_linear_recurrence(q, k, v, logf, s0, chunk):
    B, T, H, DK = q.shape
    DV = v.shape[-1]
    n = T // chunk

    def to_chunks(a):
        return a.reshape(B, n, chunk, H, a.shape[-1]).transpose(1, 0, 2, 3, 4)

    causal = jnp.tril(jnp.ones((chunk, chunk), dtype=bool))

    def step(s, xs):
        qi, ki, vi, fi = xs
        b = jnp.cumsum(fi, axis=1)
        b_last = b[:, -1:]
        q_dec = qi * jnp.exp(b)
        k_inv = ki * jnp.exp(-b)
        k_end = ki * jnp.exp(b_last - b)
        a = jnp.einsum('bthk,bshk->bhts', q_dec, k_inv)
        a = jnp.where(causal, a, 0.0)
        o = jnp.einsum('bhts,bshv->bthv', a, vi) + jnp.einsum('bthk,bhkv->bthv', q_dec, s)
        s = jnp.exp(b_last[:, 0])[..., None] * s + jnp.einsum('bshk,bshv->bhkv', k_end, vi)
        return s, o

    s_fin, o = lax.scan(step, s0, (to_chunks(q), to_chunks(k), to_chunks(v), to_chunks(logf)))
    return o.transpose(1, 0, 2, 3, 4).reshape(B, T, H, DV), s_fin


def stick_breaking_block(q, k_segs, v_segs, q_pos, k_pos, bias):
    qf = q.astype(jnp.float32) * (HEAD_DIM_B ** -0.5)
    z = jnp.concatenate([jnp.einsum('bqhd,bkhd->bhqk', qf, kk.astype(jnp.float32)) for kk in k_segs], axis=-1)
    z = z + bias.astype(jnp.float32)[None, :, None, None]
    visible = k_pos[None, :] < q_pos[:, None]
    log1mb = jnp.where(visible, jax.nn.log_sigmoid(-z), 0.0)
    rev = lax.cumsum(log1mb, axis=3, reverse=True)
    a = jnp.exp(jnp.where(visible, z + rev, -jnp.inf))
    out = 0.0
    off = 0
    for vv in v_segs:
        L = vv.shape[1]
        out = out + jnp.einsum('bhqk,bkhd->bqhd', a[..., off:off + L], vv.astype(jnp.float32))
        off += L
    return out


def stick_breaking(q, k, v, k_past, v_past, bias):
    Tq = q.shape[1]
    past = 0 if k_past is None else k_past.shape[1]
    pos = jnp.arange(past + Tq, dtype=jnp.int32)
    outs = []
    for start in range(0, Tq, Q_BLOCK):
        end = min(start + Q_BLOCK, Tq)
        k_segs = ([] if k_past is None else [k_past]) + [k[:, :end]]
        v_segs = ([] if v_past is None else [v_past]) + [v[:, :end]]
        outs.append(stick_breaking_block(q[:, start:end], k_segs, v_segs,
                                         pos[past + start:past + end], pos[:past + end], bias))
    return jnp.concatenate(outs, axis=1)


def token_mixer(h, lb, s0, k_past, v_past, chunk, w_in, g_out_a, g_out_b, b_sb, w_out):
    B, T, _ = h.shape
    proj = h @ w_in
    q_a, f_a, i_a, g_a, q_b, k_b, v_b = jnp.split(proj, SPLITS, axis=-1)

    def heads_a(t):
        return t.reshape(B, T, N_HEADS_A, HEAD_DIM_A)

    def heads_b(t):
        return t.reshape(B, T, N_HEADS_B, HEAD_DIM_B)

    f = lb + (1.0 - lb) * jax.nn.sigmoid(f_a.astype(jnp.float32))
    o_a, s_fin = gated_linear_recurrence(
        heads_a(q_a.astype(jnp.float32)) * (HEAD_DIM_A ** -0.5),
        heads_a(1.0 - f), heads_a(i_a.astype(jnp.float32)), heads_a(jnp.log(f)),
        s0.astype(jnp.float32), chunk)
    o_a = head_rmsnorm(o_a, g_out_a).astype(h.dtype) * jax.nn.silu(g_a)

    kh, vh = heads_b(k_b), heads_b(v_b)
    o_b = head_rmsnorm(stick_breaking(heads_b(q_b), kh, vh, k_past, v_past, b_sb), g_out_b).astype(h.dtype)

    out = jnp.concatenate([o_a, o_b], axis=-1) @ w_out
    return out, s_fin.astype(s0.dtype), kh, vh


def decoder_layer(x, c, lb, s0, k_past, v_past, chunk,
                  norm_ffn1, norm_mix, norm_ffn2, w_mod, b_mod,
                  w_ffn1_gate, w_ffn1_up, w_ffn1_down,
                  w_in, g_out_a, g_out_b, b_sb, w_out,
                  w_ffn2_gate, w_ffn2_up, w_ffn2_down):
    mod = jax.nn.silu(c) @ w_mod + b_mod
    sh1, sc1, ga1, sh2, sc2, ga2, sh3, sc3, ga3 = jnp.split(mod, N_MOD, axis=-1)
    h = modulate(rmsnorm(x, norm_ffn1), sh1, sc1)
    x = x + 0.5 * ga1[:, None, :] * swiglu(h, w_ffn1_gate, w_ffn1_up, w_ffn1_down)
    h = modulate(rmsnorm(x, norm_mix), sh2, sc2)
    m, s_fin, k_new, v_new = token_mixer(h, lb, s0, k_past, v_past, chunk, w_in, g_out_a, g_out_b, b_sb, w_out)
    x = x + ga2[:, None, :] * m
    h = modulate(rmsnorm(x, norm_ffn2), sh3, sc3)
    x = x + 0.5 * ga3[:, None, :] * swiglu(h, w_ffn2_gate, w_ffn2_up, w_ffn2_down)
    return x, s_fin, k_new, v_new


def final_norm(x, c, norm_final, w_final_mod, b_final_mod):
    shift, scale = jnp.split(jax.nn.silu(c) @ w_final_mod + b_final_mod, 2, axis=-1)
    return modulate(rmsnorm(x, norm_final), shift, scale)


def setup_inputs(seed: int = 0) -> dict:
    key = jax.random.key(seed)
    it = iter(jax.random.split(key, 40))

    def nrm(shape, std):
        return std * jax.random.normal(next(it), shape, dtype=jnp.float32)

    n_pages = PAST_LEN // PAGE_SIZE
    n_used = DEC_BATCH * n_pages
    n_phys = n_used + max(1, n_used // 4)
    page_table = jax.random.permutation(next(it), n_phys)[:n_used].reshape(DEC_BATCH, n_pages).astype(jnp.int32)
    sd = D_MODEL ** -0.5
    return {
        "x_prompt": nrm((BATCH, SEQ, D_MODEL), 1.0),
        "x_sample": nrm((DEC_BATCH, DEC_SEQ, D_MODEL), 1.0),
        "cache_k": nrm((DEPTH, n_phys, PAGE_SIZE, N_HEADS_B, HEAD_DIM_B), 1.0),
        "cache_v": nrm((DEPTH, n_phys, PAGE_SIZE, N_HEADS_B, HEAD_DIM_B), 1.0),
        "state_hgrn": nrm((DEPTH, DEC_BATCH, N_HEADS_A, HEAD_DIM_A, HEAD_DIM_A), 0.4),
        "page_table": page_table,
        "c_prompt": nrm((BATCH, D_MODEL), 1.0),
        "c_sample": nrm((DEC_BATCH, D_MODEL), 1.0),
        "lb_logits": nrm((DEPTH, D_A), 0.1),
        "norm_ffn1": 1.0 + nrm((DEPTH, D_MODEL), 0.02),
        "norm_mix": 1.0 + nrm((DEPTH, D_MODEL), 0.02),
        "norm_ffn2": 1.0 + nrm((DEPTH, D_MODEL), 0.02),
        "w_mod": nrm((DEPTH, D_MODEL, N_MOD * D_MODEL), 0.5 * sd),
        "b_mod": nrm((DEPTH, N_MOD * D_MODEL), 0.02),
        "w_ffn1_gate": nrm((DEPTH, D_MODEL, D_FF), sd),
        "w_ffn1_up": nrm((DEPTH, D_MODEL, D_FF), sd),
        "w_ffn1_down": nrm((DEPTH, D_FF, D_MODEL), D_FF ** -0.5),
        "w_in": nrm((DEPTH, D_MODEL, D_IN), sd),
        "g_out_a": 1.0 + nrm((DEPTH, D_A), 0.02),
        "g_out_b": 1.0 + nrm((DEPTH, D_B), 0.02),
        "b_sb": SB_BIAS_INIT + nrm((DEPTH, N_HEADS_B), 0.1),
        "w_out": nrm((DEPTH, D_MIX, D_MODEL), D_MIX ** -0.5),
        "w_ffn2_gate": nrm((DEPTH, D_MODEL, D_FF), sd),
        "w_ffn2_up": nrm((DEPTH, D_MODEL, D_FF), sd),
        "w_ffn2_down": nrm((DEPTH, D_FF, D_MODEL), D_FF ** -0.5),
        "norm_final": 1.0 + nrm((D_MODEL,), 0.02),
        "w_final_mod": nrm((D_MODEL, 2 * D_MODEL), 0.5 * sd),
        "b_final_mod": nrm((2 * D_MODEL,), 0.02),
    }


def reference(x_prompt, x_sample, cache_k, cache_v, state_hgrn, page_table, c_prompt, c_sample,
              lb_logits, norm_ffn1, norm_mix, norm_ffn2, w_mod, b_mod,
              w_ffn1_gate, w_ffn1_up, w_ffn1_down, w_in, g_out_a, g_out_b, b_sb, w_out,
              w_ffn2_gate, w_ffn2_up, w_ffn2_down, norm_final, w_final_mod, b_final_mod):
    lbs = forget_lower_bounds(lb_logits)
    xp, xs = x_prompt, x_sample
    n_dec = x_sample.shape[0]
    s0_prompt = jnp.zeros((x_prompt.shape[0], N_HEADS_A, HEAD_DIM_A, HEAD_DIM_A), x_prompt.dtype)
    chunk_prompt = min(GLA_CHUNK, x_prompt.shape[1])
    chunk_sample = x_sample.shape[1]
    kp_l, vp_l, sp_l, ks_l, vs_l, ss_l = [], [], [], [], [], []
    for l in range(DEPTH):
        lw = (norm_ffn1[l], norm_mix[l], norm_ffn2[l], w_mod[l], b_mod[l],
              w_ffn1_gate[l], w_ffn1_up[l], w_ffn1_down[l],
              w_in[l], g_out_a[l], g_out_b[l], b_sb[l], w_out[l],
              w_ffn2_gate[l], w_ffn2_up[l], w_ffn2_down[l])
        xp, sp, kp, vp = decoder_layer(xp, c_prompt, lbs[l], s0_prompt, None, None, chunk_prompt, *lw)
        k_past = jnp.take(cache_k[l], page_table, axis=0).reshape(n_dec, -1, N_HEADS_B, HEAD_DIM_B)
        v_past = jnp.take(cache_v[l], page_table, axis=0).reshape(n_dec, -1, N_HEADS_B, HEAD_DIM_B)
        xs, ss, ks, vs = decoder_layer(xs, c_sample, lbs[l], state_hgrn[l], k_past, v_past, chunk_sample, *lw)
        kp_l.append(kp); vp_l.append(vp); sp_l.append(sp)
        ks_l.append(ks); vs_l.append(vs); ss_l.append(ss)
    y_prompt = final_norm(xp, c_prompt, norm_final, w_final_mod, b_final_mod)
    y_sample = final_norm(xs, c_sample, norm_final, w_final_mod, b_final_mod)
    k_prompt = jnp.stack(kp_l)
    v_prompt = jnp.stack(vp_l)
    k_sample = jnp.stack(ks_l)
    v_sample = jnp.stack(vs_l)
    s_prompt = jnp.stack(sp_l)
    s_sample = jnp.stack(ss_l)
    return (y_prompt, y_sample, k_prompt, v_prompt, k_sample, v_sample, s_prompt, s_sample)
```

```python
import functools

import jax
import jax.numpy as jnp
from jax import lax
from jax.experimental import pallas as pl
from jax.experimental.pallas import tpu as pltpu

F32 = jnp.float32
BF16 = jnp.bfloat16

D_MODEL = 2048
D_FF = 5632
N_HEADS = 8
HEAD_DIM = 128
D_GROUP = N_HEADS * HEAD_DIM
N_MOD = 9
GLA_CHUNK = 32
PAGE = 128
EPS = 1e-6
ATT_SCALE = HEAD_DIM ** -0.5
MOD_ROWS = 16
LANE = 128
VMEM_LIMIT = 56 * 1024 * 1024

NT_DIMS = (((1,), (1,)), ((), ()))


def _params(sem):
    return pltpu.CompilerParams(dimension_semantics=sem, vmem_limit_bytes=VMEM_LIMIT)


def _sigmoid(x):
    return 1.0 / (1.0 + jnp.exp(-x))


def _silu(x):
    return x * _sigmoid(x)


def _adaln(x, nw, shift, scale):
    ms = jnp.mean(x * x, axis=-1, keepdims=True)
    y = x * lax.rsqrt(ms + EPS) * nw
    return y * (1.0 + scale) + shift


def _head_norm(o, gain):
    ms = jnp.mean(o * o, axis=-1, keepdims=True)
    return o * lax.rsqrt(ms + EPS) * gain


def _neg_softplus(z):
    return -(jnp.maximum(z, 0.0) + jnp.log1p(jnp.exp(-jnp.abs(z))))


def _split2(x):
    hi = x.astype(BF16)
    lo = (x - hi.astype(F32)).astype(BF16)
    return hi, lo


def _mod_kernel(c_ref, w_ref, b_ref, o_ref):
    a = _silu(c_ref[...]).astype(BF16)
    o_ref[...] = jnp.dot(a, w_ref[...].astype(BF16), preferred_element_type=F32) + b_ref[...]


def _modulation(c_rows, w, b, tn):
    d, n = w.shape
    return pl.pallas_call(
        _mod_kernel,
        grid=(n // tn,),
        in_specs=[
            pl.BlockSpec((MOD_ROWS, d), lambda j: (0, 0)),
            pl.BlockSpec((d, tn), lambda j: (0, j)),
            pl.BlockSpec((1, tn), lambda j: (0, j)),
        ],
        out_specs=pl.BlockSpec((MOD_ROWS, tn), lambda j: (0, j)),
        out_shape=jax.ShapeDtypeStruct((MOD_ROWS, n), F32),
        compiler_params=_params(("arbitrary",)),
        name="modulation",
    )(c_rows, w, b.reshape(1, n))


def _ffn_kernel(x_ref, nw_ref, sh_ref, sc_ref, ga_ref, wg_ref, wu_ref, wd_ref,
                nw2_ref, sh2_ref, sc2_ref, *rest, final):
    if final:
        y_ref, h_scr, acc_scr = rest
    else:
        xo_ref, y_ref, h_scr, acc_scr = rest
    f = pl.program_id(1)

    @pl.when(f == 0)
    def _():
        h = _adaln(x_ref[...], nw_ref[...], sh_ref[...], sc_ref[...])
        h_scr[...] = h.astype(BF16)
        acc_scr[...] = jnp.zeros_like(acc_scr)

    h = h_scr[...]
    g = jnp.dot(h, wg_ref[...], preferred_element_type=F32)
    u = jnp.dot(h, wu_ref[...], preferred_element_type=F32)
    a = (_silu(g) * u).astype(BF16)
    acc_scr[...] += jnp.dot(a, wd_ref[...], preferred_element_type=F32)

    @pl.when(f == pl.num_programs(1) - 1)
    def _():
        xn = x_ref[...] + 0.5 * ga_ref[...] * acc_scr[...]
        y = _adaln(xn, nw2_ref[...], sh2_ref[...], sc2_ref[...])
        if not final:
            xo_ref[...] = xn
        y_ref[...] = y.astype(y_ref.dtype)


def _ffn(x, nw, mods, wg, wu, wd, nw2, mods2, mod_spec, tm, tf, final):
    n, d = x.shape
    nf = wg.shape[1]
    row = pl.BlockSpec((tm, d), lambda i, f: (i, 0))
    vec = pl.BlockSpec((1, d), lambda i, f: (0, 0))
    in_specs = [row, vec, mod_spec, mod_spec, mod_spec,
                pl.BlockSpec((d, tf), lambda i, f: (0, f)),
                pl.BlockSpec((d, tf), lambda i, f: (0, f)),
                pl.BlockSpec((tf, d), lambda i, f: (f, 0)),
                vec, mod_spec, mod_spec]
    if final:
        out_specs = row
        out_shape = jax.ShapeDtypeStruct((n, d), F32)
    else:
        out_specs = (row, row)
        out_shape = (jax.ShapeDtypeStruct((n, d), F32), jax.ShapeDtypeStruct((n, d), BF16))
    return pl.pallas_call(
        functools.partial(_ffn_kernel, final=final),
        grid=(n // tm, nf // tf),
        in_specs=in_specs,
        out_specs=out_specs,
        out_shape=out_shape,
        scratch_shapes=[pltpu.VMEM((tm, d), BF16), pltpu.VMEM((tm, d), F32)],
        compiler_params=_params(("arbitrary", "arbitrary")),
        name="ffn_final" if final else "ffn",
    )(x, nw.reshape(1, d), mods[0], mods[1], mods[2], wg, wu, wd,
      nw2.reshape(1, d), mods2[0], mods2[1])


def _mm_kernel(h_ref, w_ref, o_ref):
    o_ref[...] = jnp.dot(h_ref[...], w_ref[...], preferred_element_type=F32)


def _in_proj(h, w, col0, ncols, tm, tn):
    n, d = h.shape
    off = col0 // tn
    return pl.pallas_call(
        _mm_kernel,
        grid=(ncols // tn, n // tm),
        in_specs=[
            pl.BlockSpec((tm, d), lambda j, i: (i, 0)),
            pl.BlockSpec((d, tn), lambda j, i: (0, j + off)),
        ],
        out_specs=pl.BlockSpec((tm, tn), lambda j, i: (i, j)),
        out_shape=jax.ShapeDtypeStruct((n, ncols), F32),
        compiler_params=_params(("arbitrary", "arbitrary")),
        name="in_proj",
    )(h, w)


def _out_proj_kernel(oa_ref, ob_ref, w_ref, x_ref, ga_ref, o_ref):
    m = jnp.dot(oa_ref[...].astype(BF16), w_ref[:D_GROUP, :], preferred_element_type=F32)
    m += jnp.dot(ob_ref[...].astype(BF16), w_ref[D_GROUP:, :], preferred_element_type=F32)
    o_ref[...] = x_ref[...] + ga_ref[...] * m


def _out_proj(oa, ob, w, x, gate, mod_spec2, tm):
    n, d = x.shape
    half = pl.BlockSpec((tm, D_GROUP), lambda i: (i, 0))
    row = pl.BlockSpec((tm, d), lambda i: (i, 0))
    return pl.pallas_call(
        _out_proj_kernel,
        grid=(n // tm,),
        in_specs=[half, half, pl.BlockSpec((2 * D_GROUP, d), lambda i: (0, 0)), row, mod_spec2],
        out_specs=row,
        out_shape=jax.ShapeDtypeStruct((n, d), F32),
        compiler_params=_params(("arbitrary",)),
        name="out_proj",
    )(oa, ob, w, x, gate)


def _gla_kernel(q_ref, f_ref, i_ref, g_ref, lbl_ref, gain_ref, s0_ref, o_ref, s_ref,
                *, rows, chunk, n_valid):
    n_groups = rows // LANE
    n_chunks = LANE // chunk
    shift = chunk.bit_length() - 1

    lbl = lbl_ref[...]
    mx = jnp.maximum(lbl, 0.0)
    e1 = jnp.exp(lbl - mx)
    lb = e1 / (e1 + jnp.exp(-mx))
    gain = gain_ref[...]

    r_id = lax.broadcasted_iota(jnp.int32, (LANE, LANE), 0)
    c_id = lax.broadcasted_iota(jnp.int32, (LANE, LANE), 1)
    same = (r_id >> shift) == (c_id >> shift)
    causal = same & (c_id <= r_id)
    m_cum = jnp.concatenate([jnp.where(causal, 1.0, 0.0), jnp.where(same, 1.0, 0.0)],
                            axis=0).astype(BF16)

    def group(gi, st):
        r = pl.ds(pl.multiple_of(gi * LANE, LANE), LANE)
        f = lb + (1.0 - lb) * _sigmoid(f_ref[r, :])
        logf = jnp.log(f)
        k = 1.0 - f
        if n_valid < LANE:
            valid = r_id < n_valid
            logf = jnp.where(valid, logf, 0.0)
            k = jnp.where(valid, k, 0.0)
        hi = logf.astype(BF16)
        r1 = logf - hi.astype(F32)
        mid = r1.astype(BF16)
        lo = (r1 - mid.astype(F32)).astype(BF16)
        bb = (jnp.dot(m_cum, hi, preferred_element_type=F32)
              + jnp.dot(m_cum, mid, preferred_element_type=F32)
              + jnp.dot(m_cum, lo, preferred_element_type=F32))
        b = bb[:LANE]
        b_last = bb[LANE:]
        v = i_ref[r, :]
        vb = v.astype(BF16)
        qd = (q_ref[r, :] * ATT_SCALE * jnp.exp(b)).astype(BF16)
        k_inv = (k * jnp.exp(-b)).astype(BF16)
        k_end = (k * jnp.exp(b_last - b)).astype(BF16)
        decay = jnp.exp(b_last)

        a = lax.dot_general(qd, k_inv, NT_DIMS, preferred_element_type=F32)
        a = jnp.where(causal, a, 0.0).astype(BF16)
        o = jnp.dot(a, vb, preferred_element_type=F32)

        v_t = v.T
        o_state = []
        for j in range(n_chunks):
            qd_j = qd[j * chunk:(j + 1) * chunk]
            o_state.append(lax.dot_general(qd_j, st.astype(BF16), NT_DIMS,
                                           preferred_element_type=F32))
            if n_chunks > 1:
                v_tj = jnp.where((c_id >> shift) == j, v_t, 0.0).astype(BF16)
            else:
                v_tj = v_t.astype(BF16)
            upd = jnp.dot(v_tj, k_end, preferred_element_type=F32)
            st = decay[j * chunk:j * chunk + 1, :] * st + upd
        o = o + (jnp.concatenate(o_state, axis=0) if n_chunks > 1 else o_state[0])

        o_ref[r, :] = (_head_norm(o, gain) * _silu(g_ref[r, :])).astype(o_ref.dtype)
        return st

    st = lax.fori_loop(0, n_groups, group, s0_ref[...].T)
    s_ref[...] = st.T


def _gla(pa, lb_logits, gain, s0, n_seq, rows, chunk, n_valid, out_dtype):
    def col(group):
        return pl.BlockSpec((rows, HEAD_DIM), lambda b, h: (b, group * N_HEADS + h))
    vec = pl.BlockSpec((1, HEAD_DIM), lambda b, h: (0, h))
    state = pl.BlockSpec((None, None, HEAD_DIM, HEAD_DIM), lambda b, h: (b, h, 0, 0))
    return pl.pallas_call(
        functools.partial(_gla_kernel, rows=rows, chunk=chunk, n_valid=n_valid),
        grid=(n_seq, N_HEADS),
        in_specs=[col(0), col(1), col(2), col(3), vec, vec, state],
        out_specs=(pl.BlockSpec((rows, HEAD_DIM), lambda b, h: (b, h)), state),
        out_shape=(jax.ShapeDtypeStruct((n_seq * rows, D_GROUP), out_dtype),
                   jax.ShapeDtypeStruct((n_seq, N_HEADS, HEAD_DIM, HEAD_DIM), F32)),
        compiler_params=_params(("arbitrary", "arbitrary")),
        name="hgrn2",
    )(pa, pa, pa, pa, lb_logits.reshape(1, D_GROUP), gain.reshape(1, D_GROUP), s0)


def _suffix_sum_matrix():
    r_id = lax.broadcasted_iota(jnp.int32, (2 * LANE, LANE), 0) & (LANE - 1)
    c_id = lax.broadcasted_iota(jnp.int32, (2 * LANE, LANE), 1)
    return jnp.where(r_id >= c_id, 1.0, 0.0).astype(BF16)


def _sb_block(q, kb, vb, bias, t2, carry, acc, vis):
    z = lax.dot_general(q, kb, NT_DIMS, preferred_element_type=F32) + bias
    l = _neg_softplus(z)
    if vis is not None:
        l = jnp.where(vis, l, 0.0)
    hi, lo = _split2(l)
    r = jnp.dot(jnp.concatenate([hi, lo], axis=1), t2, preferred_element_type=F32)
    a = jnp.exp(z + r + carry)
    if vis is not None:
        a = jnp.where(vis, a, 0.0)
    acc = acc + jnp.dot(a.astype(BF16), vb, preferred_element_type=F32)
    return carry + r[:, 0:1], acc


def _sbp_kernel(q_ref, k_ref, v_ref, bias_ref, gain_ref, o_ref):
    i = pl.program_id(2)
    q = (q_ref[...] * ATT_SCALE).astype(BF16)
    bias = bias_ref[...]
    t2 = _suffix_sum_matrix()
    r_id = lax.broadcasted_iota(jnp.int32, (LANE, LANE), 0)
    c_id = lax.broadcasted_iota(jnp.int32, (LANE, LANE), 1)

    def visit(j, carry, acc, vis):
        r = pl.ds(pl.multiple_of(j * LANE, LANE), LANE)
        return _sb_block(q, k_ref[r, :].astype(BF16), v_ref[r, :].astype(BF16), bias, t2,
                         carry, acc, vis)

    carry = jnp.zeros((LANE, 1), F32)
    acc = jnp.zeros((LANE, HEAD_DIM), F32)
    carry, acc = visit(i, carry, acc, c_id < r_id)
    carry, acc = lax.fori_loop(0, i, lambda t, c: visit(i - 1 - t, c[0], c[1], None),
                               (carry, acc))
    o_ref[...] = _head_norm(acc, gain_ref[...]).astype(o_ref.dtype)


def _sb_prompt(pq, q_col0, k, v, bias_rows, gain, n_seq, seq):
    nq = seq // LANE
    qoff = q_col0 // HEAD_DIM
    kv = pl.BlockSpec((seq, HEAD_DIM), lambda b, h, i: (b, h))
    return pl.pallas_call(
        _sbp_kernel,
        grid=(n_seq, N_HEADS, nq),
        in_specs=[
            pl.BlockSpec((LANE, HEAD_DIM), lambda b, h, i: (b * nq + i, qoff + h)),
            kv, kv,
            pl.BlockSpec((None, 1, LANE), lambda b, h, i: (h, 0, 0)),
            pl.BlockSpec((1, HEAD_DIM), lambda b, h, i: (0, h)),
        ],
        out_specs=pl.BlockSpec((LANE, HEAD_DIM), lambda b, h, i: (b * nq + i, h)),
        out_shape=jax.ShapeDtypeStruct((n_seq * seq, D_GROUP), BF16),
        compiler_params=_params(("arbitrary", "arbitrary", "arbitrary")),
        name="stickbreak_prompt",
    )(pq, k, v, bias_rows, gain.reshape(1, D_GROUP))


def _sbs_kernel(pt_ref, q_ref, kn_ref, vn_ref, kc_ref, vc_ref, bias_ref, gain_ref, o_ref,
                qbd_scr, acc_scr, car_scr, *, n_new):
    del pt_ref
    j = pl.program_id(1)
    nr = N_HEADS * n_new
    t2 = _suffix_sum_matrix()
    bias = bias_ref[...]

    def visit(kb, vb, vis):
        carry, acc = _sb_block(qbd_scr[...], kb, vb, bias, t2, car_scr[...], acc_scr[...], vis)
        car_scr[...] = carry
        acc_scr[...] = acc

    @pl.when(j == 0)
    def _():
        q = q_ref[...] * ATT_SCALE
        qt = jnp.concatenate([q] * N_HEADS, axis=0)
        r_h = lax.broadcasted_iota(jnp.int32, (nr, D_GROUP), 0) >> (n_new.bit_length() - 1)
        c_h = lax.broadcasted_iota(jnp.int32, (nr, D_GROUP), 1) >> (HEAD_DIM.bit_length() - 1)
        qbd_scr[...] = jnp.where(r_h == c_h, qt, 0.0).astype(BF16)
        car_scr[...] = jnp.zeros_like(car_scr)
        acc_scr[...] = jnp.zeros_like(acc_scr)
        pad = jnp.zeros((PAGE - n_new, D_GROUP), F32)
        kb = jnp.concatenate([kn_ref[...], pad], axis=0).astype(BF16)
        vb = jnp.concatenate([vn_ref[...], pad], axis=0).astype(BF16)
        t_id = lax.broadcasted_iota(jnp.int32, (nr, PAGE), 0) & (n_new - 1)
        s_id = lax.broadcasted_iota(jnp.int32, (nr, PAGE), 1)
        visit(kb, vb, s_id < t_id)

    visit(kc_ref[...].astype(BF16), vc_ref[...].astype(BF16), None)

    @pl.when(j == pl.num_programs(1) - 1)
    def _():
        acc = acc_scr[...]
        heads = [acc[h * n_new:(h + 1) * n_new, h * HEAD_DIM:(h + 1) * HEAD_DIM]
                 for h in range(N_HEADS)]
        heads = [o * lax.rsqrt(jnp.mean(o * o, axis=-1, keepdims=True) + EPS) for o in heads]
        o_ref[...] = jnp.concatenate(heads, axis=1) * gain_ref[...]


def _sb_sample(pq, q_col0, k_new, v_new, cache_k, cache_v, page_table, bias_col, gain,
               n_seq, n_new):
    n_pages = page_table.shape[1]
    qoff = q_col0 // D_GROUP
    nr = N_HEADS * n_new
    new = pl.BlockSpec((n_new, D_GROUP), lambda b, j, pt: (b, 0))
    page = pl.BlockSpec((None, PAGE, D_GROUP), lambda b, j, pt: (pt[b, n_pages - 1 - j], 0, 0))
    grid_spec = pltpu.PrefetchScalarGridSpec(
        num_scalar_prefetch=1,
        grid=(n_seq, n_pages),
        in_specs=[
            pl.BlockSpec((n_new, D_GROUP), lambda b, j, pt: (b, qoff)),
            new, new, page, page,
            pl.BlockSpec((nr, 1), lambda b, j, pt: (0, 0)),
            pl.BlockSpec((1, D_GROUP), lambda b, j, pt: (0, 0)),
        ],
        out_specs=new,
        scratch_shapes=[pltpu.VMEM((nr, D_GROUP), BF16), pltpu.VMEM((nr, D_GROUP), F32),
                        pltpu.VMEM((nr, 1), F32)],
    )
    return pl.pallas_call(
        functools.partial(_sbs_kernel, n_new=n_new),
        grid_spec=grid_spec,
        out_shape=jax.ShapeDtypeStruct((n_seq * n_new, D_GROUP), F32),
        compiler_params=_params(("arbitrary", "arbitrary")),
        name="stickbreak_sample",
    )(page_table, pq, k_new, v_new, cache_k, cache_v, bias_col, gain.reshape(1, D_GROUP))


def kernel(x_prompt, x_sample, cache_k, cache_v, state_hgrn, page_table, c_prompt, c_sample,
           lb_logits, norm_ffn1, norm_mix, norm_ffn2, w_mod, b_mod,
           w_ffn1_gate, w_ffn1_up, w_ffn1_down, w_in, g_out_a, g_out_b, b_sb, w_out,
           w_ffn2_gate, w_ffn2_up, w_ffn2_down, norm_final, w_final_mod, b_final_mod):
    n_p, seq, d = x_prompt.shape
    n_s, n_new, _ = x_sample.shape
    assert w_mod.shape[0] == 1, "single-layer trunk"
    assert n_p + n_s <= MOD_ROWS
    assert n_new & (n_new - 1) == 0 and n_new <= PAGE

    c_rows = jnp.concatenate(
        [c_prompt, c_sample, jnp.zeros((MOD_ROWS - n_p - n_s, d), F32)], axis=0)
    mod = _modulation(c_rows, w_mod[0], b_mod[0], 1024).reshape(MOD_ROWS, N_MOD, d)
    fmod = _modulation(c_rows, w_final_mod, b_final_mod, 1024).reshape(MOD_ROWS, 2, d)

    bf = lambda w: w[0].astype(BF16)
    wg1, wu1, wd1 = bf(w_ffn1_gate), bf(w_ffn1_up), bf(w_ffn1_down)
    wg2, wu2, wd2 = bf(w_ffn2_gate), bf(w_ffn2_up), bf(w_ffn2_down)
    w_in_b, w_out_b = bf(w_in), bf(w_out)
    bias_rows = jnp.broadcast_to(b_sb[0].reshape(N_HEADS, 1, 1), (N_HEADS, 1, LANE))
    bias_col = jnp.repeat(b_sb[0], n_new).reshape(N_HEADS * n_new, 1)

    def layer(x, mods, fmods, mod_spec, mod_spec1, tm, tm_proj, mixer):
        x1, h2 = _ffn(x, norm_ffn1[0], mods[0:3], wg1, wu1, wd1, norm_mix[0], mods[3:5],
                      mod_spec, tm, 512, final=False)
        pa = _in_proj(h2, w_in_b, 0, 5 * D_GROUP, tm_proj, 1024)
        k_new = _in_proj(h2, w_in_b, 5 * D_GROUP, D_GROUP, tm_proj, 1024)
        v_new = _in_proj(h2, w_in_b, 6 * D_GROUP, D_GROUP, tm_proj, 1024)
        oa, ob, s_fin = mixer(pa, k_new, v_new)
        x2 = _out_proj(oa, ob, w_out_b, x1, mods[5], mod_spec1, tm)
        y = _ffn(x2, norm_ffn2[0], mods[6:9], wg2, wu2, wd2, norm_final, fmods,
                 mod_spec, tm, 512, final=True)
        return y, k_new, v_new, s_fin

    tm_p = 512
    tiles_per_seq = seq // tm_p
    mods_p = [mod[:n_p, j].reshape(n_p, 1, d) for j in range(N_MOD)]
    fmods_p = [fmod[:n_p, j].reshape(n_p, 1, d) for j in range(2)]
    spec_p = pl.BlockSpec((None, 1, d), lambda i, f: (i // tiles_per_seq, 0, 0))
    spec_p1 = pl.BlockSpec((None, 1, d), lambda i: (i // tiles_per_seq, 0, 0))

    def mixer_p(pa, k_new, v_new):
        s0 = jnp.zeros((n_p, N_HEADS, HEAD_DIM, HEAD_DIM), F32)
        oa, s_fin = _gla(pa, lb_logits[0], g_out_a[0], s0, n_p, seq, GLA_CHUNK, LANE, BF16)
        ob = _sb_prompt(pa, 4 * D_GROUP, k_new, v_new, bias_rows, g_out_b[0], n_p, seq)
        return oa, ob, s_fin

    y_p, k_p, v_p, s_p = layer(x_prompt.reshape(n_p * seq, d), mods_p, fmods_p,
                               spec_p, spec_p1, tm_p, 1024, mixer_p)

    rows_s = n_s * n_new
    mods_s = [jnp.repeat(mod[n_p:n_p + n_s, j], n_new, axis=0).reshape(1, rows_s, d)
              for j in range(N_MOD)]
    fmods_s = [jnp.repeat(fmod[n_p:n_p + n_s, j], n_new, axis=0).reshape(1, rows_s, d)
               for j in range(2)]
    spec_s = pl.BlockSpec((None, rows_s, d), lambda i, f: (0, 0, 0))
    spec_s1 = pl.BlockSpec((None, rows_s, d), lambda i: (0, 0, 0))
    kc = cache_k[0].reshape(cache_k.shape[1], PAGE, D_GROUP)
    vc = cache_v[0].reshape(cache_v.shape[1], PAGE, D_GROUP)

    def mixer_s(pa, k_new, v_new):
        pa_pad = jnp.pad(pa[:, :4 * D_GROUP].reshape(n_s, n_new, 4 * D_GROUP),
                         ((0, 0), (0, LANE - n_new), (0, 0))).reshape(n_s * LANE, 4 * D_GROUP)
        oa_pad, s_fin = _gla(pa_pad, lb_logits[0], g_out_a[0], state_hgrn[0], n_s, LANE, LANE,
                             n_new, F32)
        oa = oa_pad.reshape(n_s, LANE, D_GROUP)[:, :n_new].reshape(rows_s, D_GROUP)
        ob = _sb_sample(pa, 4 * D_GROUP, k_new, v_new, kc, vc, page_table, bias_col,
                        g_out_b[0], n_s, n_new)
        return oa, ob, s_fin

    y_s, k_s, v_s, s_s = layer(x_sample.reshape(rows_s, d), mods_s, fmods_s,
                               spec_s, spec_s1, rows_s, rows_s, mixer_s)

    return (y_p.reshape(n_p, seq, d), y_s.reshape(n_s, n_new, d),
            k_p.reshape(1, n_p, seq, N_HEADS, HEAD_DIM), v_p.reshape(1, n_p, seq, N_HEADS, HEAD_DIM),
            k_s.reshape(1, n_s, n_new, N_HEADS, HEAD_DIM), v_s.reshape(1, n_s, n_new, N_HEADS, HEAD_DIM),
            s_p[None], s_s[None])
```

```python
import functools

import jax
import jax.numpy as jnp
from jax import lax
from jax.experimental import pallas as pl
from jax.experimental.pallas import tpu as pltpu

F32 = jnp.float32
BF16 = jnp.bfloat16

N_HEADS = 8
HEAD_DIM = 128
D_GROUP = N_HEADS * HEAD_DIM
N_MOD = 9
GLA_CHUNK = 32
PAGE = 128
EPS = 1e-6
ATT_SCALE = HEAD_DIM ** -0.5
MOD_ROWS = 16
LANE = 128
SB_TILE = 256
SB_PAGES = 4
VMEM_LIMIT = 56 * 1024 * 1024

NT_DIMS = (((1,), (1,)), ((), ()))


def _params(sem):
    return pltpu.CompilerParams(dimension_semantics=sem, vmem_limit_bytes=VMEM_LIMIT)


def _sigmoid(x):
    return 1.0 / (1.0 + jnp.exp(-x))


def _silu(x):
    return x * _sigmoid(x)


def _adaln(x, nw, shift, scale):
    ms = jnp.mean(x * x, axis=-1, keepdims=True)
    y = x * lax.rsqrt(ms + EPS) * nw
    return y * (1.0 + scale) + shift


def _head_norm(o, gain):
    ms = jnp.mean(o * o, axis=-1, keepdims=True)
    return o * lax.rsqrt(ms + EPS) * gain


def _split2(x):
    hi = x.astype(BF16)
    lo = (x - hi.astype(F32)).astype(BF16)
    return hi, lo


def _mod_kernel(c_ref, w_ref, b_ref, o_ref):
    a = _silu(c_ref[...]).astype(BF16)
    o_ref[...] = jnp.dot(a, w_ref[...].astype(BF16), preferred_element_type=F32) + b_ref[...]


def _modulation(c_rows, w, b, tn):
    d, n = w.shape
    return pl.pallas_call(
        _mod_kernel,
        grid=(n // tn,),
        in_specs=[
            pl.BlockSpec((MOD_ROWS, d), lambda j: (0, 0)),
            pl.BlockSpec((d, tn), lambda j: (0, j)),
            pl.BlockSpec((1, tn), lambda j: (0, j)),
        ],
        out_specs=pl.BlockSpec((MOD_ROWS, tn), lambda j: (0, j)),
        out_shape=jax.ShapeDtypeStruct((MOD_ROWS, n), F32),
        compiler_params=_params(("arbitrary",)),
        name="modulation",
    )(c_rows, w, b.reshape(1, n))


def _ffn_kernel(x_ref, nw_ref, sh_ref, sc_ref, ga_ref, wg_ref, wu_ref, wd_ref,
                nw2_ref, sh2_ref, sc2_ref, *rest, final):
    if final:
        y_ref, h_scr, acc_scr = rest
    else:
        xo_ref, y_ref, h_scr, acc_scr = rest
    f = pl.program_id(1)

    @pl.when(f == 0)
    def _():
        h = _adaln(x_ref[...], nw_ref[...], sh_ref[...], sc_ref[...])
        h_scr[...] = h.astype(BF16)
        acc_scr[...] = jnp.zeros_like(acc_scr)

    h = h_scr[...]
    g = jnp.dot(h, wg_ref[...], preferred_element_type=F32)
    u = jnp.dot(h, wu_ref[...], preferred_element_type=F32)
    a = (_silu(g) * u).astype(BF16)
    acc_scr[...] += jnp.dot(a, wd_ref[...], preferred_element_type=F32)

    @pl.when(f == pl.num_programs(1) - 1)
    def _():
        xn = x_ref[...] + 0.5 * ga_ref[...] * acc_scr[...]
        y = _adaln(xn, nw2_ref[...], sh2_ref[...], sc2_ref[...])
        if not final:
            xo_ref[...] = xn
        y_ref[...] = y.astype(y_ref.dtype)


def _ffn(x, nw, mods, wg, wu, wd, nw2, mods2, mod_spec, tm, tf, final):
    n, d = x.shape
    nf = wg.shape[1]
    row = pl.BlockSpec((tm, d), lambda i, f: (i, 0))
    vec = pl.BlockSpec((1, d), lambda i, f: (0, 0))
    in_specs = [row, vec, mod_spec, mod_spec, mod_spec,
                pl.BlockSpec((d, tf), lambda i, f: (0, f)),
                pl.BlockSpec((d, tf), lambda i, f: (0, f)),
                pl.BlockSpec((tf, d), lambda i, f: (f, 0)),
                vec, mod_spec, mod_spec]
    if final:
        out_specs = row
        out_shape = jax.ShapeDtypeStruct((n, d), F32)
    else:
        out_specs = (row, row)
        out_shape = (jax.ShapeDtypeStruct((n, d), F32), jax.ShapeDtypeStruct((n, d), BF16))
    return pl.pallas_call(
        functools.partial(_ffn_kernel, final=final),
        grid=(n // tm, nf // tf),
        in_specs=in_specs,
        out_specs=out_specs,
        out_shape=out_shape,
        scratch_shapes=[pltpu.VMEM((tm, d), BF16), pltpu.VMEM((tm, d), F32)],
        compiler_params=_params(("arbitrary", "arbitrary")),
        name="ffn_final" if final else "ffn",
    )(x, nw.reshape(1, d), mods[0], mods[1], mods[2], wg, wu, wd,
      nw2.reshape(1, d), mods2[0], mods2[1])


def _mm_kernel(h_ref, w_ref, o_ref):
    o_ref[...] = jnp.dot(h_ref[...], w_ref[...], preferred_element_type=F32)


def _in_proj(h, w, col0, ncols, tm, tn):
    n, d = h.shape
    off = col0 // tn
    return pl.pallas_call(
        _mm_kernel,
        grid=(ncols // tn, n // tm),
        in_specs=[
            pl.BlockSpec((tm, d), lambda j, i: (i, 0)),
            pl.BlockSpec((d, tn), lambda j, i: (0, j + off)),
        ],
        out_specs=pl.BlockSpec((tm, tn), lambda j, i: (i, j)),
        out_shape=jax.ShapeDtypeStruct((n, ncols), F32),
        compiler_params=_params(("arbitrary", "arbitrary")),
        name="in_proj",
    )(h, w)


def _out_proj_kernel(oa_ref, ob_ref, w_ref, x_ref, ga_ref, o_ref):
    m = jnp.dot(oa_ref[...].astype(BF16), w_ref[:D_GROUP, :], preferred_element_type=F32)
    m += jnp.dot(ob_ref[...].astype(BF16), w_ref[D_GROUP:, :], preferred_element_type=F32)
    o_ref[...] = x_ref[...] + ga_ref[...] * m


def _out_proj(oa, ob, w, x, gate, mod_spec2, tm):
    n, d = x.shape
    half = pl.BlockSpec((tm, D_GROUP), lambda i: (i, 0))
    row = pl.BlockSpec((tm, d), lambda i: (i, 0))
    return pl.pallas_call(
        _out_proj_kernel,
        grid=(n // tm,),
        in_specs=[half, half, pl.BlockSpec((2 * D_GROUP, d), lambda i: (0, 0)), row, mod_spec2],
        out_specs=row,
        out_shape=jax.ShapeDtypeStruct((n, d), F32),
        compiler_params=_params(("arbitrary",)),
        name="out_proj",
    )(oa, ob, w, x, gate)


def _gla_kernel(q_ref, f_ref, i_ref, g_ref, lbl_ref, gain_ref, s0_ref, o_ref, s_ref, st_scr,
                *, chunk, n_valid):
    t = pl.program_id(1)
    n_chunks = LANE // chunk
    shift = chunk.bit_length() - 1

    @pl.when(t == 0)
    def _():
        for h in range(N_HEADS):
            st_scr[h] = s0_ref[h].T

    r_id = lax.broadcasted_iota(jnp.int32, (LANE, LANE), 0)
    c_id = lax.broadcasted_iota(jnp.int32, (LANE, LANE), 1)
    same = (r_id >> shift) == (c_id >> shift)
    causal = same & (c_id <= r_id)
    m_cum = jnp.concatenate([jnp.where(causal, 1.0, 0.0), jnp.where(same, 1.0, 0.0)],
                            axis=0).astype(BF16)

    for h in range(N_HEADS):
        c = slice(h * HEAD_DIM, (h + 1) * HEAD_DIM)
        lbl = lbl_ref[:, c]
        mx = jnp.maximum(lbl, 0.0)
        e1 = jnp.exp(lbl - mx)
        lb = e1 / (e1 + jnp.exp(-mx))

        f = lb + (1.0 - lb) * _sigmoid(f_ref[:, c])
        logf = jnp.log(f)
        k = 1.0 - f
        if n_valid < LANE:
            valid = r_id < n_valid
            logf = jnp.where(valid, logf, 0.0)
            k = jnp.where(valid, k, 0.0)
        hi = logf.astype(BF16)
        r1 = logf - hi.astype(F32)
        mid = r1.astype(BF16)
        lo = (r1 - mid.astype(F32)).astype(BF16)
        bb = (jnp.dot(m_cum, hi, preferred_element_type=F32)
              + jnp.dot(m_cum, mid, preferred_element_type=F32)
              + jnp.dot(m_cum, lo, preferred_element_type=F32))
        b = bb[:LANE]
        b_last = bb[LANE:]
        v = i_ref[:, c]
        vb = v.astype(BF16)
        qd = (q_ref[:, c] * ATT_SCALE * jnp.exp(b)).astype(BF16)
        k_inv = (k * jnp.exp(-b)).astype(BF16)
        k_end = (k * jnp.exp(b_last - b)).astype(BF16)
        decay = jnp.exp(b_last)

        a = lax.dot_general(qd, k_inv, NT_DIMS, preferred_element_type=F32)
        a = jnp.where(causal, a, 0.0).astype(BF16)
        o = jnp.dot(a, vb, preferred_element_type=F32)

        v_t = v.T
        st = st_scr[h]
        o_state = []
        for j in range(n_chunks):
            qd_j = qd[j * chunk:(j + 1) * chunk]
            o_state.append(lax.dot_general(qd_j, st.astype(BF16), NT_DIMS,
                                           preferred_element_type=F32))
            if n_chunks > 1:
                v_tj = jnp.where((c_id >> shift) == j, v_t, 0.0).astype(BF16)
            else:
                v_tj = v_t.astype(BF16)
            upd = jnp.dot(v_tj, k_end, preferred_element_type=F32)
            st = decay[j * chunk:j * chunk + 1, :] * st + upd
        st_scr[h] = st
        o = o + (jnp.concatenate(o_state, axis=0) if n_chunks > 1 else o_state[0])
        o_ref[:, c] = (_head_norm(o, gain_ref[:, c]) * _silu(g_ref[:, c])).astype(o_ref.dtype)

    @pl.when(t == pl.num_programs(1) - 1)
    def _():
        for h in range(N_HEADS):
            s_ref[h] = st_scr[h].T


def _gla(pa, lb_logits, gain, s0, n_seq, rows, chunk, n_valid, out_dtype):
    n_groups = rows // LANE

    def col(group):
        return pl.BlockSpec((LANE, D_GROUP), lambda b, t: (b * n_groups + t, group))
    vec = pl.BlockSpec((1, D_GROUP), lambda b, t: (0, 0))
    state = pl.BlockSpec((None, N_HEADS, HEAD_DIM, HEAD_DIM), lambda b, t: (b, 0, 0, 0))
    return pl.pallas_call(
        functools.partial(_gla_kernel, chunk=chunk, n_valid=n_valid),
        grid=(n_seq, n_groups),
        in_specs=[col(0), col(1), col(2), col(3), vec, vec, state],
        out_specs=(pl.BlockSpec((LANE, D_GROUP), lambda b, t: (b * n_groups + t, 0)), state),
        out_shape=(jax.ShapeDtypeStruct((n_seq * rows, D_GROUP), out_dtype),
                   jax.ShapeDtypeStruct((n_seq, N_HEADS, HEAD_DIM, HEAD_DIM), F32)),
        scratch_shapes=[pltpu.VMEM((N_HEADS, HEAD_DIM, HEAD_DIM), F32)],
        compiler_params=_params(("arbitrary", "arbitrary")),
        name="hgrn2",
    )(pa, pa, pa, pa, lb_logits.reshape(1, D_GROUP), gain.reshape(1, D_GROUP), s0)


def _suffix_weights():
    r_id = lax.broadcasted_iota(jnp.int32, (2 * LANE, 2 * LANE), 0) & (LANE - 1)
    c_id = lax.broadcasted_iota(jnp.int32, (2 * LANE, 2 * LANE), 1)
    return jnp.where((r_id >= c_id) | (c_id >= LANE), 1.0, 0.0).astype(BF16)


def _softplus(z):
    return jnp.maximum(z, 0.0) + jnp.log(1.0 + jnp.exp(-jnp.abs(z)))


def _sb_blocks(z_blocks, vis_blocks, w2, run):
    m = z_blocks[0].shape[0]
    sp = []
    for z, vis in zip(z_blocks, vis_blocks):
        s = _softplus(z)
        sp.append(s if vis is None else jnp.where(vis, s, 0.0))
    hi, lo = _split2(sp[0] if len(sp) == 1 else jnp.concatenate(sp, axis=0))
    rt = jnp.dot(jnp.concatenate([hi, lo], axis=1), w2, preferred_element_type=F32)
    out = []
    for n, (z, vis) in enumerate(zip(z_blocks, vis_blocks)):
        within = rt[n * m:(n + 1) * m, :LANE]
        total = rt[n * m:(n + 1) * m, LANE:]
        a = jnp.exp(z - within if run is None else z - within - run)
        out.append(a if vis is None else jnp.where(vis, a, 0.0))
        run = total if run is None else run + total
    return out, run


def _sbp_kernel(q_ref, k_ref, v_ref, bias_ref, gain_ref, o_ref, q_scr, acc_scr, run_scr):
    i = pl.program_id(1)
    w2 = _suffix_weights()
    q_scr[...] = (q_ref[...] * ATT_SCALE).astype(BF16)
    acc_scr[...] = jnp.zeros_like(acc_scr)
    run_scr[...] = jnp.zeros_like(run_scr)
    r_id = lax.broadcasted_iota(jnp.int32, (SB_TILE, LANE), 0)
    c_id = lax.broadcasted_iota(jnp.int32, (SB_TILE, LANE), 1)
    n_blk = SB_TILE // LANE

    def slab(j, diagonal):
        r = pl.ds(pl.multiple_of(j * SB_TILE, SB_TILE), SB_TILE)
        for h in range(N_HEADS):
            c = slice(h * HEAD_DIM, (h + 1) * HEAD_DIM)
            kb = k_ref[r, c].astype(BF16)
            vb = v_ref[r, c].astype(BF16)
            z = lax.dot_general(q_scr[:, c], kb, NT_DIMS, preferred_element_type=F32) + bias_ref[h]
            order = range(n_blk - 1, -1, -1)
            z_blocks = [z[:, n * LANE:(n + 1) * LANE] for n in order]
            if diagonal:
                vis_blocks = [c_id + n * LANE < r_id for n in order]
            else:
                vis_blocks = [None] * n_blk
            a_blocks, total = _sb_blocks(z_blocks, vis_blocks, w2, None)
            a = jnp.concatenate(a_blocks[::-1], axis=1).astype(BF16)
            p = jnp.dot(a, vb, preferred_element_type=F32)
            run = run_scr[h]
            acc_scr[h] += p * jnp.exp(-run)
            run_scr[h] = run + total

    slab(i, True)

    def body(t, carry):
        slab(i - 1 - t, False)
        return carry
    lax.fori_loop(0, i, body, 0)

    for h in range(N_HEADS):
        c = slice(h * HEAD_DIM, (h + 1) * HEAD_DIM)
        o_ref[:, c] = _head_norm(acc_scr[h], gain_ref[:, c]).astype(o_ref.dtype)


def _sb_prompt(pq, q_col0, k, v, bias_rows, gain, n_seq, seq):
    nq = seq // SB_TILE
    qoff = q_col0 // D_GROUP
    kv = pl.BlockSpec((seq, D_GROUP), lambda b, i: (b, 0))
    return pl.pallas_call(
        _sbp_kernel,
        grid=(n_seq, nq),
        in_specs=[
            pl.BlockSpec((SB_TILE, D_GROUP), lambda b, i: (b * nq + i, qoff)),
            kv, kv,
            pl.BlockSpec((N_HEADS, 1, SB_TILE), lambda b, i: (0, 0, 0)),
            pl.BlockSpec((1, D_GROUP), lambda b, i: (0, 0)),
        ],
        out_specs=pl.BlockSpec((SB_TILE, D_GROUP), lambda b, i: (b * nq + i, 0)),
        out_shape=jax.ShapeDtypeStruct((n_seq * seq, D_GROUP), BF16),
        scratch_shapes=[pltpu.VMEM((SB_TILE, D_GROUP), BF16),
                        pltpu.VMEM((N_HEADS, SB_TILE, HEAD_DIM), F32),
                        pltpu.VMEM((N_HEADS, SB_TILE, LANE), F32)],
        compiler_params=_params(("arbitrary", "arbitrary")),
        name="stickbreak_prompt",
    )(pq, k, v, bias_rows, gain.reshape(1, D_GROUP))


def _sbs_kernel(pt_ref, q_ref, kn_ref, vn_ref, *rest, n_new):
    del pt_ref
    kc_refs = rest[:SB_PAGES]
    vc_refs = rest[SB_PAGES:2 * SB_PAGES]
    bias_ref, gain_ref, o_ref, qbd_scr, acc_scr, run_scr = rest[2 * SB_PAGES:]
    j = pl.program_id(1)
    nr = N_HEADS * n_new
    w2 = _suffix_weights()

    def visit(k_pages, v_pages, vis_blocks):
        qbd = qbd_scr[...]
        z_blocks = [lax.dot_general(qbd, kb, NT_DIMS, preferred_element_type=F32) + bias_ref[...]
                    for kb in k_pages]
        a_blocks, run = _sb_blocks(z_blocks, vis_blocks, w2, run_scr[...])
        acc = acc_scr[...]
        for a, vb in zip(a_blocks, v_pages):
            acc += jnp.dot(a.astype(BF16), vb, preferred_element_type=F32)
        acc_scr[...] = acc
        run_scr[...] = run

    @pl.when(j == 0)
    def _():
        q = q_ref[...] * ATT_SCALE
        qt = jnp.concatenate([q] * N_HEADS, axis=0)
        r_h = lax.broadcasted_iota(jnp.int32, (nr, D_GROUP), 0) >> (n_new.bit_length() - 1)
        c_h = lax.broadcasted_iota(jnp.int32, (nr, D_GROUP), 1) >> (HEAD_DIM.bit_length() - 1)
        qbd_scr[...] = jnp.where(r_h == c_h, qt, 0.0).astype(BF16)
        run_scr[...] = jnp.zeros_like(run_scr)
        acc_scr[...] = jnp.zeros_like(acc_scr)
        pad = jnp.zeros((PAGE - n_new, D_GROUP), F32)
        kb = jnp.concatenate([kn_ref[...], pad], axis=0).astype(BF16)
        vb = jnp.concatenate([vn_ref[...], pad], axis=0).astype(BF16)
        t_id = lax.broadcasted_iota(jnp.int32, (nr, PAGE), 0) & (n_new - 1)
        s_id = lax.broadcasted_iota(jnp.int32, (nr, PAGE), 1)
        visit([kb], [vb], [s_id < t_id])

    visit([r[...].astype(BF16) for r in kc_refs], [r[...].astype(BF16) for r in vc_refs],
          [None] * SB_PAGES)

    @pl.when(j == pl.num_programs(1) - 1)
    def _():
        acc = acc_scr[...]
        heads = [acc[h * n_new:(h + 1) * n_new, h * HEAD_DIM:(h + 1) * HEAD_DIM]
                 for h in range(N_HEADS)]
        heads = [o * lax.rsqrt(jnp.mean(o * o, axis=-1, keepdims=True) + EPS) for o in heads]
        o_ref[...] = jnp.concatenate(heads, axis=1) * gain_ref[...]


def _sb_sample(pq, q_col0, k_new, v_new, cache_k, cache_v, page_table, bias_rep, gain,
               n_seq, n_new):
    n_pages = page_table.shape[1]
    qoff = q_col0 // D_GROUP
    nr = N_HEADS * n_new
    new = pl.BlockSpec((n_new, D_GROUP), lambda b, j, pt: (b, 0))

    def page(r):
        return pl.BlockSpec((None, PAGE, D_GROUP),
                            lambda b, j, pt: (pt[b, n_pages - 1 - (j * SB_PAGES + r)], 0, 0))
    pages = [page(r) for r in range(SB_PAGES)]
    grid_spec = pltpu.PrefetchScalarGridSpec(
        num_scalar_prefetch=1,
        grid=(n_seq, n_pages // SB_PAGES),
        in_specs=[pl.BlockSpec((n_new, D_GROUP), lambda b, j, pt: (b, qoff)), new, new,
                  *pages, *pages,
                  pl.BlockSpec((nr, LANE), lambda b, j, pt: (0, 0)),
                  pl.BlockSpec((1, D_GROUP), lambda b, j, pt: (0, 0))],
        out_specs=new,
        scratch_shapes=[pltpu.VMEM((nr, D_GROUP), BF16), pltpu.VMEM((nr, D_GROUP), F32),
                        pltpu.VMEM((nr, LANE), F32)],
    )
    return pl.pallas_call(
        functools.partial(_sbs_kernel, n_new=n_new),
        grid_spec=grid_spec,
        out_shape=jax.ShapeDtypeStruct((n_seq * n_new, D_GROUP), F32),
        compiler_params=_params(("arbitrary", "arbitrary")),
        name="stickbreak_sample",
    )(page_table, pq, k_new, v_new, *([cache_k] * SB_PAGES), *([cache_v] * SB_PAGES),
      bias_rep, gain.reshape(1, D_GROUP))


def kernel(x_prompt, x_sample, cache_k, cache_v, state_hgrn, page_table, c_prompt, c_sample,
           lb_logits, norm_ffn1, norm_mix, norm_ffn2, w_mod, b_mod,
           w_ffn1_gate, w_ffn1_up, w_ffn1_down, w_in, g_out_a, g_out_b, b_sb, w_out,
           w_ffn2_gate, w_ffn2_up, w_ffn2_down, norm_final, w_final_mod, b_final_mod):
    n_p, seq, d = x_prompt.shape
    n_s, n_new, _ = x_sample.shape
    assert w_mod.shape[0] == 1, "single-layer trunk"
    assert n_p + n_s <= MOD_ROWS
    assert n_new & (n_new - 1) == 0 and n_new <= PAGE
    assert page_table.shape[1] % SB_PAGES == 0 and seq % SB_TILE == 0

    c_rows = jnp.concatenate(
        [c_prompt, c_sample, jnp.zeros((MOD_ROWS - n_p - n_s, d), F32)], axis=0)
    mod = _modulation(c_rows, w_mod[0], b_mod[0], 1024).reshape(MOD_ROWS, N_MOD, d)
    fmod = _modulation(c_rows, w_final_mod, b_final_mod, 1024).reshape(MOD_ROWS, 2, d)

    bf = lambda w: w[0].astype(BF16)
    wg1, wu1, wd1 = bf(w_ffn1_gate), bf(w_ffn1_up), bf(w_ffn1_down)
    wg2, wu2, wd2 = bf(w_ffn2_gate), bf(w_ffn2_up), bf(w_ffn2_down)
    w_in_b, w_out_b = bf(w_in), bf(w_out)
    bias_rows = jnp.broadcast_to(b_sb[0].reshape(N_HEADS, 1, 1), (N_HEADS, 1, SB_TILE))
    bias_rep = jnp.broadcast_to(jnp.repeat(b_sb[0], n_new)[:, None], (N_HEADS * n_new, LANE))

    def layer(x, mods, fmods, mod_spec, mod_spec1, tm, tm_proj, mixer):
        x1, h2 = _ffn(x, norm_ffn1[0], mods[0:3], wg1, wu1, wd1, norm_mix[0], mods[3:5],
                      mod_spec, tm, 512, final=False)
        pa = _in_proj(h2, w_in_b, 0, 5 * D_GROUP, tm_proj, 1024)
        k_new = _in_proj(h2, w_in_b, 5 * D_GROUP, D_GROUP, tm_proj, 1024)
        v_new = _in_proj(h2, w_in_b, 6 * D_GROUP, D_GROUP, tm_proj, 1024)
        oa, ob, s_fin = mixer(pa, k_new, v_new)
        x2 = _out_proj(oa, ob, w_out_b, x1, mods[5], mod_spec1, tm)
        y = _ffn(x2, norm_ffn2[0], mods[6:9], wg2, wu2, wd2, norm_final, fmods,
                 mod_spec, tm, 512, final=True)
        return y, k_new, v_new, s_fin

    tm_p = 512
    tiles_per_seq = seq // tm_p
    mods_p = [mod[:n_p, j].reshape(n_p, 1, d) for j in range(N_MOD)]
    fmods_p = [fmod[:n_p, j].reshape(n_p, 1, d) for j in range(2)]
    spec_p = pl.BlockSpec((None, 1, d), lambda i, f: (i // tiles_per_seq, 0, 0))
    spec_p1 = pl.BlockSpec((None, 1, d), lambda i: (i // tiles_per_seq, 0, 0))

    def mixer_p(pa, k_new, v_new):
        s0 = jnp.zeros((n_p, N_HEADS, HEAD_DIM, HEAD_DIM), F32)
        oa, s_fin = _gla(pa, lb_logits[0], g_out_a[0], s0, n_p, seq, GLA_CHUNK, LANE, BF16)
        ob = _sb_prompt(pa, 4 * D_GROUP, k_new, v_new, bias_rows, g_out_b[0], n_p, seq)
        return oa, ob, s_fin

    y_p, k_p, v_p, s_p = layer(x_prompt.reshape(n_p * seq, d), mods_p, fmods_p,
                               spec_p, spec_p1, tm_p, 1024, mixer_p)

    rows_s = n_s * n_new
    mods_s = [jnp.repeat(mod[n_p:n_p + n_s, j], n_new, axis=0).reshape(1, rows_s, d)
              for j in range(N_MOD)]
    fmods_s = [jnp.repeat(fmod[n_p:n_p + n_s, j], n_new, axis=0).reshape(1, rows_s, d)
               for j in range(2)]
    spec_s = pl.BlockSpec((None, rows_s, d), lambda i, f: (0, 0, 0))
    spec_s1 = pl.BlockSpec((None, rows_s, d), lambda i: (0, 0, 0))
    kc = cache_k.reshape(cache_k.shape[1], PAGE, D_GROUP)
    vc = cache_v.reshape(cache_v.shape[1], PAGE, D_GROUP)

    def mixer_s(pa, k_new, v_new):
        pa_pad = jnp.pad(pa[:, :4 * D_GROUP].reshape(n_s, n_new, 4 * D_GROUP),
                         ((0, 0), (0, LANE - n_new), (0, 0))).reshape(n_s * LANE, 4 * D_GROUP)
        oa_pad, s_fin = _gla(pa_pad, lb_logits[0], g_out_a[0], state_hgrn[0], n_s, LANE, LANE,
                             n_new, F32)
        oa = oa_pad.reshape(n_s, LANE, D_GROUP)[:, :n_new].reshape(rows_s, D_GROUP)
        ob = _sb_sample(pa, 4 * D_GROUP, k_new, v_new, kc, vc, page_table, bias_rep,
                        g_out_b[0], n_s, n_new)
        return oa, ob, s_fin

    y_s, k_s, v_s, s_s = layer(x_sample.reshape(rows_s, d), mods_s, fmods_s,
                               spec_s, spec_s1, rows_s, rows_s, mixer_s)

    return (y_p.reshape(n_p, seq, d), y_s.reshape(n_s, n_new, d),
            k_p.reshape(1, n_p, seq, N_HEADS, HEAD_DIM), v_p.reshape(1, n_p, seq, N_HEADS, HEAD_DIM),
            k_s.reshape(1, n_s, n_new, N_HEADS, HEAD_DIM), v_s.reshape(1, n_s, n_new, N_HEADS, HEAD_DIM),
            s_p[None], s_s[None])
```

```python
import functools

import jax
import jax.numpy as jnp
from jax import lax
from jax.experimental import pallas as pl
from jax.experimental.pallas import tpu as pltpu

F32 = jnp.float32
BF16 = jnp.bfloat16

N_HEADS = 8
HEAD_DIM = 128
D_GROUP = N_HEADS * HEAD_DIM
N_MOD = 9
GLA_CHUNK = 32
PAGE = 128
EPS = 1e-6
ATT_SCALE = HEAD_DIM ** -0.5
MOD_ROWS = 16
LANE = 128
SB_TILE = 256
SB_PAGES = 4
VMEM_LIMIT = 56 * 1024 * 1024

NT_DIMS = (((1,), (1,)), ((), ()))


def _params(sem):
    return pltpu.CompilerParams(dimension_semantics=sem, vmem_limit_bytes=VMEM_LIMIT)


def _sigmoid(x):
    return 1.0 / (1.0 + jnp.exp(-x))


def _silu(x):
    return x * _sigmoid(x)


def _adaln(x, nw, shift, scale):
    ms = jnp.mean(x * x, axis=-1, keepdims=True)
    y = x * lax.rsqrt(ms + EPS) * nw
    return y * (1.0 + scale) + shift


def _head_norm(o, gain):
    ms = jnp.mean(o * o, axis=-1, keepdims=True)
    return o * lax.rsqrt(ms + EPS) * gain


def _split2(x):
    hi = x.astype(BF16)
    lo = (x - hi.astype(F32)).astype(BF16)
    return hi, lo


def _mod_kernel(c_ref, w_ref, b_ref, o_ref):
    a = _silu(c_ref[...]).astype(BF16)
    o_ref[...] = jnp.dot(a, w_ref[...].astype(BF16), preferred_element_type=F32) + b_ref[...]


def _modulation(c_rows, w, b, tn):
    d, n = w.shape
    return pl.pallas_call(
        _mod_kernel,
        grid=(n // tn,),
        in_specs=[
            pl.BlockSpec((MOD_ROWS, d), lambda j: (0, 0)),
            pl.BlockSpec((d, tn), lambda j: (0, j)),
            pl.BlockSpec((1, tn), lambda j: (0, j)),
        ],
        out_specs=pl.BlockSpec((MOD_ROWS, tn), lambda j: (0, j)),
        out_shape=jax.ShapeDtypeStruct((MOD_ROWS, n), F32),
        compiler_params=_params(("arbitrary",)),
        name="modulation",
    )(c_rows, w, b.reshape(1, n))


def _ffn_kernel(x_ref, nw_ref, sh_ref, sc_ref, ga_ref, wg_ref, wu_ref, wd_ref,
                nw2_ref, sh2_ref, sc2_ref, *rest, final):
    if final:
        y_ref, h_scr, acc_scr = rest
    else:
        xo_ref, y_ref, h_scr, acc_scr = rest
    f = pl.program_id(1)

    @pl.when(f == 0)
    def _():
        h = _adaln(x_ref[...], nw_ref[...], sh_ref[...], sc_ref[...])
        h_scr[...] = h.astype(BF16)
        acc_scr[...] = jnp.zeros_like(acc_scr)

    h = h_scr[...]
    g = jnp.dot(h, wg_ref[...], preferred_element_type=F32)
    u = jnp.dot(h, wu_ref[...], preferred_element_type=F32)
    a = (_silu(g) * u).astype(BF16)
    acc_scr[...] += jnp.dot(a, wd_ref[...], preferred_element_type=F32)

    @pl.when(f == pl.num_programs(1) - 1)
    def _():
        xn = x_ref[...] + 0.5 * ga_ref[...] * acc_scr[...]
        y = _adaln(xn, nw2_ref[...], sh2_ref[...], sc2_ref[...])
        if not final:
            xo_ref[...] = xn
        y_ref[...] = y.astype(y_ref.dtype)


def _ffn(x, nw, mods, wg, wu, wd, nw2, mods2, mod_spec, tm, tf, final):
    n, d = x.shape
    nf = wg.shape[1]
    row = pl.BlockSpec((tm, d), lambda i, f: (i, 0))
    vec = pl.BlockSpec((1, d), lambda i, f: (0, 0))
    in_specs = [row, vec, mod_spec, mod_spec, mod_spec,
                pl.BlockSpec((d, tf), lambda i, f: (0, f)),
                pl.BlockSpec((d, tf), lambda i, f: (0, f)),
                pl.BlockSpec((tf, d), lambda i, f: (f, 0)),
                vec, mod_spec, mod_spec]
    if final:
        out_specs = row
        out_shape = jax.ShapeDtypeStruct((n, d), F32)
    else:
        out_specs = (row, row)
        out_shape = (jax.ShapeDtypeStruct((n, d), F32), jax.ShapeDtypeStruct((n, d), BF16))
    return pl.pallas_call(
        functools.partial(_ffn_kernel, final=final),
        grid=(n // tm, nf // tf),
        in_specs=in_specs,
        out_specs=out_specs,
        out_shape=out_shape,
        scratch_shapes=[pltpu.VMEM((tm, d), BF16), pltpu.VMEM((tm, d), F32)],
        compiler_params=_params(("arbitrary", "arbitrary")),
        name="ffn_final" if final else "ffn",
    )(x, nw.reshape(1, d), mods[0], mods[1], mods[2], wg, wu, wd,
      nw2.reshape(1, d), mods2[0], mods2[1])


def _mm_kernel(h_ref, w_ref, o_ref):
    o_ref[...] = jnp.dot(h_ref[...], w_ref[...], preferred_element_type=F32)


def _in_proj(h, w, col0, ncols, tm, tn):
    n, d = h.shape
    off = col0 // tn
    return pl.pallas_call(
        _mm_kernel,
        grid=(ncols // tn, n // tm),
        in_specs=[
            pl.BlockSpec((tm, d), lambda j, i: (i, 0)),
            pl.BlockSpec((d, tn), lambda j, i: (0, j + off)),
        ],
        out_specs=pl.BlockSpec((tm, tn), lambda j, i: (i, j)),
        out_shape=jax.ShapeDtypeStruct((n, ncols), F32),
        compiler_params=_params(("arbitrary", "arbitrary")),
        name="in_proj",
    )(h, w)


def _out_proj_kernel(oa_ref, ob_ref, w_ref, x_ref, ga_ref, o_ref):
    m = jnp.dot(oa_ref[...].astype(BF16), w_ref[:D_GROUP, :], preferred_element_type=F32)
    m += jnp.dot(ob_ref[...].astype(BF16), w_ref[D_GROUP:, :], preferred_element_type=F32)
    o_ref[...] = x_ref[...] + ga_ref[...] * m


def _out_proj(oa, ob, w, x, gate, mod_spec2, tm):
    n, d = x.shape
    half = pl.BlockSpec((tm, D_GROUP), lambda i: (i, 0))
    row = pl.BlockSpec((tm, d), lambda i: (i, 0))
    return pl.pallas_call(
        _out_proj_kernel,
        grid=(n // tm,),
        in_specs=[half, half, pl.BlockSpec((2 * D_GROUP, d), lambda i: (0, 0)), row, mod_spec2],
        out_specs=row,
        out_shape=jax.ShapeDtypeStruct((n, d), F32),
        compiler_params=_params(("arbitrary",)),
        name="out_proj",
    )(oa, ob, w, x, gate)


def _gla_kernel(q_ref, f_ref, i_ref, g_ref, lbl_ref, gain_ref, s0_ref, o_ref, s_ref, st_scr,
                *, chunk, n_valid):
    t = pl.program_id(1)
    n_chunks = LANE // chunk
    shift = chunk.bit_length() - 1

    @pl.when(t == 0)
    def _():
        for h in range(N_HEADS):
            st_scr[h] = s0_ref[h].T

    r_id = lax.broadcasted_iota(jnp.int32, (LANE, LANE), 0)
    c_id = lax.broadcasted_iota(jnp.int32, (LANE, LANE), 1)
    same = (r_id >> shift) == (c_id >> shift)
    causal = same & (c_id <= r_id)
    m_cum = jnp.concatenate([jnp.where(causal, 1.0, 0.0), jnp.where(same, 1.0, 0.0)],
                            axis=0).astype(BF16)

    heads = [slice(h * HEAD_DIM, (h + 1) * HEAD_DIM) for h in range(N_HEADS)]
    lbl = lbl_ref[...]
    mx = jnp.maximum(lbl, 0.0)
    e1 = jnp.exp(lbl - mx)
    lb = e1 / (e1 + jnp.exp(-mx))

    f = lb + (1.0 - lb) * _sigmoid(f_ref[...])
    logf = jnp.log(f)
    k = 1.0 - f
    if n_valid < LANE:
        valid = lax.broadcasted_iota(jnp.int32, (LANE, D_GROUP), 0) < n_valid
        logf = jnp.where(valid, logf, 0.0)
        k = jnp.where(valid, k, 0.0)
    hi = logf.astype(BF16)
    r1 = logf - hi.astype(F32)
    mid = r1.astype(BF16)
    lo = (r1 - mid.astype(F32)).astype(BF16)
    bb = jnp.dot(m_cum, jnp.concatenate([hi, mid, lo], axis=1), preferred_element_type=F32)
    bb = bb[:, :D_GROUP] + bb[:, D_GROUP:2 * D_GROUP] + bb[:, 2 * D_GROUP:]
    b = bb[:LANE]
    b_last = bb[LANE:]
    v = i_ref[...]
    vb = v.astype(BF16)
    qd = (q_ref[...] * ATT_SCALE * jnp.exp(b)).astype(BF16)
    k_inv = (k * jnp.exp(-b)).astype(BF16)
    k_end = (k * jnp.exp(b_last - b)).astype(BF16)
    decay = jnp.exp(b_last)

    a = [lax.dot_general(qd[:, c], k_inv[:, c], NT_DIMS, preferred_element_type=F32)
         for c in heads]
    a = [jnp.where(causal, x, 0.0).astype(BF16) for x in a]
    o = [jnp.dot(x, vb[:, c], preferred_element_type=F32) for x, c in zip(a, heads)]
    v_t = [v[:, c].T for c in heads]
    st = [st_scr[h] for h in range(N_HEADS)]
    o_state = [[] for _ in heads]
    for j in range(n_chunks):
        rows = slice(j * chunk, (j + 1) * chunk)
        for h, c in enumerate(heads):
            o_state[h].append(lax.dot_general(qd[rows, c], st[h].astype(BF16), NT_DIMS,
                                              preferred_element_type=F32))
            if n_chunks > 1:
                v_tj = jnp.where((c_id >> shift) == j, v_t[h], 0.0).astype(BF16)
            else:
                v_tj = v_t[h].astype(BF16)
            upd = jnp.dot(v_tj, k_end[:, c], preferred_element_type=F32)
            st[h] = decay[j * chunk:j * chunk + 1, c] * st[h] + upd
    for h, c in enumerate(heads):
        st_scr[h] = st[h]
        o_h = o[h] + (jnp.concatenate(o_state[h], axis=0) if n_chunks > 1 else o_state[h][0])
        o_ref[:, c] = (_head_norm(o_h, gain_ref[:, c]) * _silu(g_ref[:, c])).astype(o_ref.dtype)

    @pl.when(t == pl.num_programs(1) - 1)
    def _():
        for h in range(N_HEADS):
            s_ref[h] = st_scr[h].T


def _gla(pa, lb_logits, gain, s0, n_seq, rows, chunk, n_valid, out_dtype):
    n_groups = rows // LANE

    def col(group):
        return pl.BlockSpec((LANE, D_GROUP), lambda b, t: (b * n_groups + t, group))
    vec = pl.BlockSpec((1, D_GROUP), lambda b, t: (0, 0))
    state = pl.BlockSpec((None, N_HEADS, HEAD_DIM, HEAD_DIM), lambda b, t: (b, 0, 0, 0))
    return pl.pallas_call(
        functools.partial(_gla_kernel, chunk=chunk, n_valid=n_valid),
        grid=(n_seq, n_groups),
        in_specs=[col(0), col(1), col(2), col(3), vec, vec, state],
        out_specs=(pl.BlockSpec((LANE, D_GROUP), lambda b, t: (b * n_groups + t, 0)), state),
        out_shape=(jax.ShapeDtypeStruct((n_seq * rows, D_GROUP), out_dtype),
                   jax.ShapeDtypeStruct((n_seq, N_HEADS, HEAD_DIM, HEAD_DIM), F32)),
        scratch_shapes=[pltpu.VMEM((N_HEADS, HEAD_DIM, HEAD_DIM), F32)],
        compiler_params=_params(("arbitrary", "arbitrary")),
        name="hgrn2",
    )(pa, pa, pa, pa, lb_logits.reshape(1, D_GROUP), gain.reshape(1, D_GROUP), s0)


def _suffix_weights():
    r_id = lax.broadcasted_iota(jnp.int32, (2 * LANE, 2 * LANE), 0) & (LANE - 1)
    c_id = lax.broadcasted_iota(jnp.int32, (2 * LANE, 2 * LANE), 1)
    return jnp.where((r_id >= c_id) | (c_id >= LANE), 1.0, 0.0).astype(BF16)


def _softplus(z):
    return jnp.maximum(z, 0.0) + jnp.log(1.0 + jnp.exp(-jnp.abs(z)))


def _sb_chains(chains, w2):
    m = chains[0][0][0].shape[0]
    sp = []
    for z_blocks, vis_blocks, _ in chains:
        for z, vis in zip(z_blocks, vis_blocks):
            s = _softplus(z)
            sp.append(s if vis is None else jnp.where(vis, s, 0.0))
    hi, lo = _split2(sp[0] if len(sp) == 1 else jnp.concatenate(sp, axis=0))
    rt = jnp.dot(jnp.concatenate([hi, lo], axis=1), w2, preferred_element_type=F32)
    results = []
    n = 0
    for z_blocks, vis_blocks, run in chains:
        out = []
        for z, vis in zip(z_blocks, vis_blocks):
            within = rt[n * m:(n + 1) * m, :LANE]
            total = rt[n * m:(n + 1) * m, LANE:]
            a = jnp.exp(z - within if run is None else z - within - run)
            out.append(a if vis is None else jnp.where(vis, a, 0.0))
            run = total if run is None else run + total
            n += 1
        results.append((out, run))
    return results


def _sbp_kernel(q_ref, k_ref, v_ref, bias_ref, gain_ref, o_ref, q_scr, acc_scr, run_scr):
    i = pl.program_id(1)
    w2 = _suffix_weights()
    q_scr[...] = (q_ref[...] * ATT_SCALE).astype(BF16)
    acc_scr[...] = jnp.zeros_like(acc_scr)
    run_scr[...] = jnp.zeros_like(run_scr)
    r_id = lax.broadcasted_iota(jnp.int32, (SB_TILE, LANE), 0)
    c_id = lax.broadcasted_iota(jnp.int32, (SB_TILE, LANE), 1)
    n_blk = SB_TILE // LANE

    def slab(j, diagonal):
        r = pl.ds(pl.multiple_of(j * SB_TILE, SB_TILE), SB_TILE)
        order = range(n_blk - 1, -1, -1)
        if diagonal:
            vis_blocks = [c_id + n * LANE < r_id for n in order]
        else:
            vis_blocks = [None] * n_blk
        chains = []
        for h in range(N_HEADS):
            c = slice(h * HEAD_DIM, (h + 1) * HEAD_DIM)
            kb = k_ref[r, c].astype(BF16)
            z = lax.dot_general(q_scr[:, c], kb, NT_DIMS, preferred_element_type=F32) + bias_ref[h]
            chains.append(([z[:, n * LANE:(n + 1) * LANE] for n in order], vis_blocks, None))
        for h, (a_blocks, total) in enumerate(_sb_chains(chains, w2)):
            c = slice(h * HEAD_DIM, (h + 1) * HEAD_DIM)
            a = jnp.concatenate(a_blocks[::-1], axis=1).astype(BF16)
            p = jnp.dot(a, v_ref[r, c].astype(BF16), preferred_element_type=F32)
            run = run_scr[h]
            acc_scr[h] += p * jnp.exp(-run)
            run_scr[h] = run + total

    slab(i, True)

    def body(t, carry):
        slab(i - 1 - t, False)
        return carry
    lax.fori_loop(0, i, body, 0)

    for h in range(N_HEADS):
        c = slice(h * HEAD_DIM, (h + 1) * HEAD_DIM)
        o_ref[:, c] = _head_norm(acc_scr[h], gain_ref[:, c]).astype(o_ref.dtype)


def _sb_prompt(pq, q_col0, k, v, bias_rows, gain, n_seq, seq):
    nq = seq // SB_TILE
    qoff = q_col0 // D_GROUP
    kv = pl.BlockSpec((seq, D_GROUP), lambda b, i: (b, 0))
    return pl.pallas_call(
        _sbp_kernel,
        grid=(n_seq, nq),
        in_specs=[
            pl.BlockSpec((SB_TILE, D_GROUP), lambda b, i: (b * nq + i, qoff)),
            kv, kv,
            pl.BlockSpec((N_HEADS, 1, SB_TILE), lambda b, i: (0, 0, 0)),
            pl.BlockSpec((1, D_GROUP), lambda b, i: (0, 0)),
        ],
        out_specs=pl.BlockSpec((SB_TILE, D_GROUP), lambda b, i: (b * nq + i, 0)),
        out_shape=jax.ShapeDtypeStruct((n_seq * seq, D_GROUP), BF16),
        scratch_shapes=[pltpu.VMEM((SB_TILE, D_GROUP), BF16),
                        pltpu.VMEM((N_HEADS, SB_TILE, HEAD_DIM), F32),
                        pltpu.VMEM((N_HEADS, SB_TILE, LANE), F32)],
        compiler_params=_params(("arbitrary", "arbitrary")),
        name="stickbreak_prompt",
    )(pq, k, v, bias_rows, gain.reshape(1, D_GROUP))


def _sbs_kernel(pt_ref, q_ref, kn_ref, vn_ref, *rest, n_new):
    del pt_ref
    kc_refs = rest[:SB_PAGES]
    vc_refs = rest[SB_PAGES:2 * SB_PAGES]
    bias_ref, gain_ref, o_ref, q_scr, acc_scr, run_scr = rest[2 * SB_PAGES:]
    j = pl.program_id(1)
    nr = N_HEADS * n_new
    w2 = _suffix_weights()
    row = lax.broadcasted_iota(jnp.int32, (nr, LANE), 0)
    col = lax.broadcasted_iota(jnp.int32, (nr, LANE), 1)
    own = (col & (N_HEADS - 1)) == (row >> (n_new.bit_length() - 1))

    def visit(k_rows, v_rows, vis):
        q = q_scr[...]
        z_blocks = []
        for kb in k_rows:
            z = lax.dot_general(q, kb, NT_DIMS, preferred_element_type=F32)
            z_blocks += [z[:, n * LANE:(n + 1) * LANE] + bias_ref[...]
                         for n in range(kb.shape[0] // LANE - 1, -1, -1)]
        (a_blocks, run), = _sb_chains([(z_blocks, [vis] * len(z_blocks), run_scr[...])], w2)
        acc = acc_scr[...]
        first = 0
        for vb in v_rows:
            nb = vb.shape[0] // LANE
            a = a_blocks[first:first + nb][::-1]
            a = (a[0] if nb == 1 else jnp.concatenate(a, axis=1)).astype(BF16)
            acc += jnp.dot(a, vb, preferred_element_type=F32)
            first += nb
        acc_scr[...] = acc
        run_scr[...] = run

    @pl.when(j == 0)
    def _():
        q = q_ref[...] * ATT_SCALE
        q_scr[...] = jnp.concatenate(
            [q[:, h * HEAD_DIM:(h + 1) * HEAD_DIM] for h in range(N_HEADS)], axis=0).astype(BF16)
        run_scr[...] = jnp.zeros_like(run_scr)
        acc_scr[...] = jnp.zeros_like(acc_scr)
        pad = jnp.zeros((LANE - nr, HEAD_DIM), F32)
        kb = jnp.concatenate([kn_ref[...], pad], axis=0).astype(BF16)
        vb = jnp.concatenate([vn_ref[...], pad], axis=0).astype(BF16)
        earlier = (col >> (N_HEADS.bit_length() - 1)) < (row & (n_new - 1))
        visit([kb], [vb], own & earlier)

    rows = PAGE * N_HEADS
    visit([r[...].reshape(rows, HEAD_DIM).astype(BF16) for r in kc_refs],
          [r[...].reshape(rows, HEAD_DIM).astype(BF16) for r in vc_refs], own)

    @pl.when(j == pl.num_programs(1) - 1)
    def _():
        acc = acc_scr[...]
        heads = [acc[h * n_new:(h + 1) * n_new] for h in range(N_HEADS)]
        heads = [o * lax.rsqrt(jnp.mean(o * o, axis=-1, keepdims=True) + EPS) for o in heads]
        o_ref[...] = jnp.concatenate(heads, axis=1) * gain_ref[...]


def _sb_sample(pq, q_col0, k_new, v_new, cache_k, cache_v, page_table, bias_rep, gain,
               n_seq, n_new):
    n_pages = page_table.shape[1]
    qoff = q_col0 // D_GROUP
    nr = N_HEADS * n_new
    assert nr <= LANE
    new = pl.BlockSpec((None, nr, HEAD_DIM), lambda b, j, pt: (b, 0, 0))

    def page(r):
        return pl.BlockSpec(
            (None, None, PAGE, N_HEADS, HEAD_DIM),
            lambda b, j, pt: (0, pt[b, n_pages - 1 - (j * SB_PAGES + r)], 0, 0, 0))
    pages = [page(r) for r in range(SB_PAGES)]
    grid_spec = pltpu.PrefetchScalarGridSpec(
        num_scalar_prefetch=1,
        grid=(n_seq, n_pages // SB_PAGES),
        in_specs=[pl.BlockSpec((n_new, D_GROUP), lambda b, j, pt: (b, qoff)), new, new,
                  *pages, *pages,
                  pl.BlockSpec((nr, LANE), lambda b, j, pt: (0, 0)),
                  pl.BlockSpec((1, D_GROUP), lambda b, j, pt: (0, 0))],
        out_specs=pl.BlockSpec((n_new, D_GROUP), lambda b, j, pt: (b, 0)),
        scratch_shapes=[pltpu.VMEM((nr, HEAD_DIM), BF16), pltpu.VMEM((nr, HEAD_DIM), F32),
                        pltpu.VMEM((nr, LANE), F32)],
    )
    return pl.pallas_call(
        functools.partial(_sbs_kernel, n_new=n_new),
        grid_spec=grid_spec,
        out_shape=jax.ShapeDtypeStruct((n_seq * n_new, D_GROUP), F32),
        compiler_params=_params(("arbitrary", "arbitrary")),
        name="stickbreak_sample",
    )(page_table, pq, k_new, v_new, *([cache_k] * SB_PAGES), *([cache_v] * SB_PAGES),
      bias_rep, gain.reshape(1, D_GROUP))


def kernel(x_prompt, x_sample, cache_k, cache_v, state_hgrn, page_table, c_prompt, c_sample,
           lb_logits, norm_ffn1, norm_mix, norm_ffn2, w_mod, b_mod,
           w_ffn1_gate, w_ffn1_up, w_ffn1_down, w_in, g_out_a, g_out_b, b_sb, w_out,
           w_ffn2_gate, w_ffn2_up, w_ffn2_down, norm_final, w_final_mod, b_final_mod):
    n_p, seq, d = x_prompt.shape
    n_s, n_new, _ = x_sample.shape
    assert w_mod.shape[0] == 1, "single-layer trunk"
    assert n_p + n_s <= MOD_ROWS
    assert n_new & (n_new - 1) == 0 and n_new <= PAGE
    assert page_table.shape[1] % SB_PAGES == 0 and seq % SB_TILE == 0

    c_rows = jnp.concatenate(
        [c_prompt, c_sample, jnp.zeros((MOD_ROWS - n_p - n_s, d), F32)], axis=0)
    mod = _modulation(c_rows, w_mod[0], b_mod[0], 1024).reshape(MOD_ROWS, N_MOD, d)
    fmod = _modulation(c_rows, w_final_mod, b_final_mod, 1024).reshape(MOD_ROWS, 2, d)

    bf = lambda w: w[0].astype(BF16)
    wg1, wu1, wd1 = bf(w_ffn1_gate), bf(w_ffn1_up), bf(w_ffn1_down)
    wg2, wu2, wd2 = bf(w_ffn2_gate), bf(w_ffn2_up), bf(w_ffn2_down)
    w_in_b, w_out_b = bf(w_in), bf(w_out)
    bias_rows = jnp.broadcast_to(b_sb[0].reshape(N_HEADS, 1, 1), (N_HEADS, 1, SB_TILE))
    bias_rep = jnp.broadcast_to(jnp.repeat(b_sb[0], n_new)[:, None], (N_HEADS * n_new, LANE))

    def layer(x, mods, fmods, mod_spec, mod_spec1, tm, tm_proj, mixer):
        x1, h2 = _ffn(x, norm_ffn1[0], mods[0:3], wg1, wu1, wd1, norm_mix[0], mods[3:5],
                      mod_spec, tm, 512, final=False)
        pa = _in_proj(h2, w_in_b, 0, 5 * D_GROUP, tm_proj, 1024)
        k_new = _in_proj(h2, w_in_b, 5 * D_GROUP, D_GROUP, tm_proj, 1024)
        v_new = _in_proj(h2, w_in_b, 6 * D_GROUP, D_GROUP, tm_proj, 1024)
        oa, ob, s_fin = mixer(pa, k_new, v_new)
        x2 = _out_proj(oa, ob, w_out_b, x1, mods[5], mod_spec1, tm)
        y = _ffn(x2, norm_ffn2[0], mods[6:9], wg2, wu2, wd2, norm_final, fmods,
                 mod_spec, tm, 512, final=True)
        return y, k_new, v_new, s_fin

    tm_p = 512
    tiles_per_seq = seq // tm_p
    mods_p = [mod[:n_p, j].reshape(n_p, 1, d) for j in range(N_MOD)]
    fmods_p = [fmod[:n_p, j].reshape(n_p, 1, d) for j in range(2)]
    spec_p = pl.BlockSpec((None, 1, d), lambda i, f: (i // tiles_per_seq, 0, 0))
    spec_p1 = pl.BlockSpec((None, 1, d), lambda i: (i // tiles_per_seq, 0, 0))

    def mixer_p(pa, k_new, v_new):
        s0 = jnp.zeros((n_p, N_HEADS, HEAD_DIM, HEAD_DIM), F32)
        oa, s_fin = _gla(pa, lb_logits[0], g_out_a[0], s0, n_p, seq, GLA_CHUNK, LANE, BF16)
        ob = _sb_prompt(pa, 4 * D_GROUP, k_new, v_new, bias_rows, g_out_b[0], n_p, seq)
        return oa, ob, s_fin

    y_p, k_p, v_p, s_p = layer(x_prompt.reshape(n_p * seq, d), mods_p, fmods_p,
                               spec_p, spec_p1, tm_p, 1024, mixer_p)

    rows_s = n_s * n_new
    mods_s = [jnp.repeat(mod[n_p:n_p + n_s, j], n_new, axis=0).reshape(1, rows_s, d)
              for j in range(N_MOD)]
    fmods_s = [jnp.repeat(fmod[n_p:n_p + n_s, j], n_new, axis=0).reshape(1, rows_s, d)
               for j in range(2)]
    spec_s = pl.BlockSpec((None, rows_s, d), lambda i, f: (0, 0, 0))
    spec_s1 = pl.BlockSpec((None, rows_s, d), lambda i: (0, 0, 0))

    def mixer_s(pa, k_new, v_new):
        pa_pad = jnp.pad(pa[:, :4 * D_GROUP].reshape(n_s, n_new, 4 * D_GROUP),
                         ((0, 0), (0, LANE - n_new), (0, 0))).reshape(n_s * LANE, 4 * D_GROUP)
        oa_pad, s_fin = _gla(pa_pad, lb_logits[0], g_out_a[0], state_hgrn[0], n_s, LANE, LANE,
                             n_new, F32)
        oa = oa_pad.reshape(n_s, LANE, D_GROUP)[:, :n_new].reshape(rows_s, D_GROUP)
        rows_th = (n_s, n_new * N_HEADS, HEAD_DIM)
        ob = _sb_sample(pa, 4 * D_GROUP, k_new.reshape(rows_th), v_new.reshape(rows_th),
                        cache_k, cache_v, page_table, bias_rep, g_out_b[0], n_s, n_new)
        return oa, ob, s_fin

    y_s, k_s, v_s, s_s = layer(x_sample.reshape(rows_s, d), mods_s, fmods_s,
                               spec_s, spec_s1, rows_s, rows_s, mixer_s)

    return (y_p.reshape(n_p, seq, d), y_s.reshape(n_s, n_new, d),
            k_p.reshape(1, n_p, seq, N_HEADS, HEAD_DIM), v_p.reshape(1, n_p, seq, N_HEADS, HEAD_DIM),
            k_s.reshape(1, n_s, n_new, N_HEADS, HEAD_DIM), v_s.reshape(1, n_s, n_new, N_HEADS, HEAD_DIM),
            s_p[None], s_s[None])
```

```python
import functools

import jax
import jax.numpy as jnp
from jax import lax
from jax.experimental import pallas as pl
from jax.experimental.pallas import tpu as pltpu

F32 = jnp.float32
BF16 = jnp.bfloat16

N_HEADS = 8
HEAD_DIM = 128
D_GROUP = N_HEADS * HEAD_DIM
N_MOD = 9
GLA_CHUNK = 32
PAGE = 128
EPS = 1e-6
ATT_SCALE = HEAD_DIM ** -0.5
LOG2E = 1.4426950408889634
MOD_ROWS = 16
LANE = 128
SB_TILE = 256
SB_PAGES = 8
VMEM_LIMIT = 56 * 1024 * 1024

NT_DIMS = (((1,), (1,)), ((), ()))


def _params(sem):
    return pltpu.CompilerParams(dimension_semantics=sem, vmem_limit_bytes=VMEM_LIMIT)


def _sigmoid(x):
    return 1.0 / (1.0 + jnp.exp(-x))


def _silu(x):
    return x * _sigmoid(x)


def _adaln(x, nw, shift, scale):
    ms = jnp.mean(x * x, axis=-1, keepdims=True)
    y = x * lax.rsqrt(ms + EPS) * nw
    return y * (1.0 + scale) + shift


def _head_norm(o, gain):
    ms = jnp.mean(o * o, axis=-1, keepdims=True)
    return o * lax.rsqrt(ms + EPS) * gain


def _split2(x):
    hi = x.astype(BF16)
    lo = (x - hi.astype(F32)).astype(BF16)
    return hi, lo


def _mod_kernel(c_ref, w_ref, b_ref, o_ref):
    a = _silu(c_ref[...]).astype(BF16)
    o_ref[...] = jnp.dot(a, w_ref[...].astype(BF16), preferred_element_type=F32) + b_ref[...]


def _modulation(c_rows, w, b, tn):
    d, n = w.shape
    return pl.pallas_call(
        _mod_kernel,
        grid=(n // tn,),
        in_specs=[
            pl.BlockSpec((MOD_ROWS, d), lambda j: (0, 0)),
            pl.BlockSpec((d, tn), lambda j: (0, j)),
            pl.BlockSpec((1, tn), lambda j: (0, j)),
        ],
        out_specs=pl.BlockSpec((MOD_ROWS, tn), lambda j: (0, j)),
        out_shape=jax.ShapeDtypeStruct((MOD_ROWS, n), F32),
        compiler_params=_params(("arbitrary",)),
        name="modulation",
    )(c_rows, w, b.reshape(1, n))


def _ffn_kernel(x_ref, nw_ref, sh_ref, sc_ref, ga_ref, wg_ref, wu_ref, wd_ref,
                nw2_ref, sh2_ref, sc2_ref, *rest, final):
    if final:
        y_ref, h_scr, acc_scr = rest
    else:
        xo_ref, y_ref, h_scr, acc_scr = rest
    f = pl.program_id(1)

    @pl.when(f == 0)
    def _():
        h = _adaln(x_ref[...], nw_ref[...], sh_ref[...], sc_ref[...])
        h_scr[...] = h.astype(BF16)
        acc_scr[...] = jnp.zeros_like(acc_scr)

    h = h_scr[...]
    g = jnp.dot(h, wg_ref[...], preferred_element_type=F32)
    u = jnp.dot(h, wu_ref[...], preferred_element_type=F32)
    a = (_silu(g) * u).astype(BF16)
    acc_scr[...] += jnp.dot(a, wd_ref[...], preferred_element_type=F32)

    @pl.when(f == pl.num_programs(1) - 1)
    def _():
        xn = x_ref[...] + 0.5 * ga_ref[...] * acc_scr[...]
        y = _adaln(xn, nw2_ref[...], sh2_ref[...], sc2_ref[...])
        if not final:
            xo_ref[...] = xn
        y_ref[...] = y.astype(y_ref.dtype)


def _ffn(x, nw, mods, wg, wu, wd, nw2, mods2, mod_spec, tm, tf, final):
    n, d = x.shape
    nf = wg.shape[1]
    row = pl.BlockSpec((tm, d), lambda i, f: (i, 0))
    vec = pl.BlockSpec((1, d), lambda i, f: (0, 0))
    in_specs = [row, vec, mod_spec, mod_spec, mod_spec,
                pl.BlockSpec((d, tf), lambda i, f: (0, f)),
                pl.BlockSpec((d, tf), lambda i, f: (0, f)),
                pl.BlockSpec((tf, d), lambda i, f: (f, 0)),
                vec, mod_spec, mod_spec]
    if final:
        out_specs = row
        out_shape = jax.ShapeDtypeStruct((n, d), F32)
    else:
        out_specs = (row, row)
        out_shape = (jax.ShapeDtypeStruct((n, d), F32), jax.ShapeDtypeStruct((n, d), BF16))
    return pl.pallas_call(
        functools.partial(_ffn_kernel, final=final),
        grid=(n // tm, nf // tf),
        in_specs=in_specs,
        out_specs=out_specs,
        out_shape=out_shape,
        scratch_shapes=[pltpu.VMEM((tm, d), BF16), pltpu.VMEM((tm, d), F32)],
        compiler_params=_params(("arbitrary", "arbitrary")),
        name="ffn_final" if final else "ffn",
    )(x, nw.reshape(1, d), mods[0], mods[1], mods[2], wg, wu, wd,
      nw2.reshape(1, d), mods2[0], mods2[1])


def _mm_kernel(h_ref, w_ref, o_ref):
    o_ref[...] = jnp.dot(h_ref[...], w_ref[...], preferred_element_type=F32)


def _in_proj(h, w, col0, ncols, tm, tn):
    n, d = h.shape
    off = col0 // tn
    return pl.pallas_call(
        _mm_kernel,
        grid=(ncols // tn, n // tm),
        in_specs=[
            pl.BlockSpec((tm, d), lambda j, i: (i, 0)),
            pl.BlockSpec((d, tn), lambda j, i: (0, j + off)),
        ],
        out_specs=pl.BlockSpec((tm, tn), lambda j, i: (i, j)),
        out_shape=jax.ShapeDtypeStruct((n, ncols), F32),
        compiler_params=_params(("arbitrary", "arbitrary")),
        name="in_proj",
    )(h, w)


def _out_proj_kernel(oa_ref, ob_ref, w_ref, x_ref, ga_ref, o_ref):
    m = jnp.dot(oa_ref[...].astype(BF16), w_ref[:D_GROUP, :], preferred_element_type=F32)
    m += jnp.dot(ob_ref[...].astype(BF16), w_ref[D_GROUP:, :], preferred_element_type=F32)
    o_ref[...] = x_ref[...] + ga_ref[...] * m


def _out_proj(oa, ob, w, x, gate, mod_spec2, tm):
    n, d = x.shape
    half = pl.BlockSpec((tm, D_GROUP), lambda i: (i, 0))
    row = pl.BlockSpec((tm, d), lambda i: (i, 0))
    return pl.pallas_call(
        _out_proj_kernel,
        grid=(n // tm,),
        in_specs=[half, half, pl.BlockSpec((2 * D_GROUP, d), lambda i: (0, 0)), row, mod_spec2],
        out_specs=row,
        out_shape=jax.ShapeDtypeStruct((n, d), F32),
        compiler_params=_params(("arbitrary",)),
        name="out_proj",
    )(oa, ob, w, x, gate)


def _gla_kernel(q_ref, f_ref, i_ref, g_ref, lbl_ref, gain_ref, s0_ref, o_ref, s_ref, st_scr,
                *, chunk, n_valid):
    t = pl.program_id(1)
    n_chunks = LANE // chunk
    shift = chunk.bit_length() - 1

    @pl.when(t == 0)
    def _():
        for h in range(N_HEADS):
            st_scr[h] = s0_ref[h].T

    r_id = lax.broadcasted_iota(jnp.int32, (LANE, LANE), 0)
    c_id = lax.broadcasted_iota(jnp.int32, (LANE, LANE), 1)
    same = (r_id >> shift) == (c_id >> shift)
    causal = same & (c_id <= r_id)
    m_cum = jnp.concatenate([jnp.where(causal, 1.0, 0.0), jnp.where(same, 1.0, 0.0)],
                            axis=0).astype(BF16)

    heads = [slice(h * HEAD_DIM, (h + 1) * HEAD_DIM) for h in range(N_HEADS)]
    lbl = lbl_ref[...]
    mx = jnp.maximum(lbl, 0.0)
    e1 = jnp.exp(lbl - mx)
    lb = e1 / (e1 + jnp.exp(-mx))

    f = lb + (1.0 - lb) * _sigmoid(f_ref[...])
    logf = jnp.log(f)
    k = 1.0 - f
    if n_valid < LANE:
        valid = lax.broadcasted_iota(jnp.int32, (LANE, D_GROUP), 0) < n_valid
        logf = jnp.where(valid, logf, 0.0)
        k = jnp.where(valid, k, 0.0)
    hi = logf.astype(BF16)
    r1 = logf - hi.astype(F32)
    mid = r1.astype(BF16)
    lo = (r1 - mid.astype(F32)).astype(BF16)
    bb = jnp.dot(m_cum, jnp.concatenate([hi, mid, lo], axis=1), preferred_element_type=F32)
    bb = bb[:, :D_GROUP] + bb[:, D_GROUP:2 * D_GROUP] + bb[:, 2 * D_GROUP:]
    b = bb[:LANE]
    b_last = bb[LANE:]
    v = i_ref[...]
    vb = v.astype(BF16)
    qd = (q_ref[...] * ATT_SCALE * jnp.exp(b)).astype(BF16)
    k_inv = (k * jnp.exp(-b)).astype(BF16)
    k_end = (k * jnp.exp(b_last - b)).astype(BF16)
    decay = jnp.exp(b_last)

    a = [lax.dot_general(qd[:, c], k_inv[:, c], NT_DIMS, preferred_element_type=F32)
         for c in heads]
    a = [jnp.where(causal, x, 0.0).astype(BF16) for x in a]
    o = [jnp.dot(x, vb[:, c], preferred_element_type=F32) for x, c in zip(a, heads)]
    v_t = [v[:, c].T for c in heads]
    st = [st_scr[h] for h in range(N_HEADS)]
    o_state = [[] for _ in heads]
    for j in range(n_chunks):
        rows = slice(j * chunk, (j + 1) * chunk)
        for h, c in enumerate(heads):
            o_state[h].append(lax.dot_general(qd[rows, c], st[h].astype(BF16), NT_DIMS,
                                              preferred_element_type=F32))
            if n_chunks > 1:
                v_tj = jnp.where((c_id >> shift) == j, v_t[h], 0.0).astype(BF16)
            else:
                v_tj = v_t[h].astype(BF16)
            upd = jnp.dot(v_tj, k_end[:, c], preferred_element_type=F32)
            st[h] = decay[j * chunk:j * chunk + 1, c] * st[h] + upd
    for h, c in enumerate(heads):
        st_scr[h] = st[h]
        o_h = o[h] + (jnp.concatenate(o_state[h], axis=0) if n_chunks > 1 else o_state[h][0])
        o_ref[:, c] = (_head_norm(o_h, gain_ref[:, c]) * _silu(g_ref[:, c])).astype(o_ref.dtype)

    @pl.when(t == pl.num_programs(1) - 1)
    def _():
        for h in range(N_HEADS):
            s_ref[h] = st_scr[h].T


def _gla(pa, lb_logits, gain, s0, n_seq, rows, chunk, n_valid, out_dtype):
    n_groups = rows // LANE

    def col(group):
        return pl.BlockSpec((LANE, D_GROUP), lambda b, t: (b * n_groups + t, group))
    vec = pl.BlockSpec((1, D_GROUP), lambda b, t: (0, 0))
    state = pl.BlockSpec((None, N_HEADS, HEAD_DIM, HEAD_DIM), lambda b, t: (b, 0, 0, 0))
    return pl.pallas_call(
        functools.partial(_gla_kernel, chunk=chunk, n_valid=n_valid),
        grid=(n_seq, n_groups),
        in_specs=[col(0), col(1), col(2), col(3), vec, vec, state],
        out_specs=(pl.BlockSpec((LANE, D_GROUP), lambda b, t: (b * n_groups + t, 0)), state),
        out_shape=(jax.ShapeDtypeStruct((n_seq * rows, D_GROUP), out_dtype),
                   jax.ShapeDtypeStruct((n_seq, N_HEADS, HEAD_DIM, HEAD_DIM), F32)),
        scratch_shapes=[pltpu.VMEM((N_HEADS, HEAD_DIM, HEAD_DIM), F32)],
        compiler_params=_params(("arbitrary", "arbitrary")),
        name="hgrn2",
    )(pa, pa, pa, pa, lb_logits.reshape(1, D_GROUP), gain.reshape(1, D_GROUP), s0)


def _suffix_weights():
    r_id = lax.broadcasted_iota(jnp.int32, (2 * LANE, 2 * LANE), 0) & (LANE - 1)
    c_id = lax.broadcasted_iota(jnp.int32, (2 * LANE, 2 * LANE), 1)
    return jnp.where((r_id >= c_id) | (c_id >= LANE), 1.0, 0.0).astype(BF16)


def _softplus2(z):
    return jnp.maximum(z, 0.0) + jnp.log(1.0 + jnp.exp2(-jnp.abs(z))) * LOG2E


def _sb_softplus(z_blocks, vis_blocks):
    sp = []
    for z, vis in zip(z_blocks, vis_blocks):
        s = _softplus2(z)
        sp.append(s if vis is None else jnp.where(vis, s, 0.0))
    hi, lo = _split2(sp[0] if len(sp) == 1 else jnp.concatenate(sp, axis=0))
    return jnp.concatenate([hi, lo], axis=1)


def _sb_weights(z_blocks, vis_blocks, rt, run):
    m = z_blocks[0].shape[0]
    out = []
    for n, (z, vis) in enumerate(zip(z_blocks, vis_blocks)):
        within = rt[n * m:(n + 1) * m, :LANE]
        total = rt[n * m:(n + 1) * m, LANE:]
        a = jnp.exp2(z - within if run is None else z - within - run)
        out.append(a if vis is None else jnp.where(vis, a, 0.0))
        run = total if run is None else run + total
    return out, run


def _skewed(n_items, stages):
    for tick in range(n_items + len(stages) - 1):
        for s in range(len(stages) - 1, -1, -1):
            i = tick - s
            if 0 <= i < n_items:
                stages[s](i)


def _sbp_kernel(q_ref, k_ref, v_ref, bias_ref, gain_ref, o_ref, q_scr, acc_scr, run_scr):
    i = pl.program_id(1)
    w2 = _suffix_weights()
    q_scr[...] = (q_ref[...] * (ATT_SCALE * LOG2E)).astype(BF16)
    acc_scr[...] = jnp.zeros_like(acc_scr)
    run_scr[...] = jnp.zeros_like(run_scr)
    r_id = lax.broadcasted_iota(jnp.int32, (SB_TILE, LANE), 0)
    c_id = lax.broadcasted_iota(jnp.int32, (SB_TILE, LANE), 1)
    n_blk = SB_TILE // LANE

    def slab(j, diagonal):
        r = pl.ds(pl.multiple_of(j * SB_TILE, SB_TILE), SB_TILE)
        order = range(n_blk - 1, -1, -1)
        if diagonal:
            vis_blocks = [c_id + n * LANE < r_id for n in order]
        else:
            vis_blocks = [None] * n_blk
        heads = [slice(h * HEAD_DIM, (h + 1) * HEAD_DIM) for h in range(N_HEADS)]
        zb, hl, rt = [None] * N_HEADS, [None] * N_HEADS, [None] * N_HEADS

        def scores(h):
            kb = k_ref[r, heads[h]].astype(BF16)
            z = lax.dot_general(q_scr[:, heads[h]], kb, NT_DIMS,
                                preferred_element_type=F32) + bias_ref[h]
            zb[h] = [z[:, n * LANE:(n + 1) * LANE] for n in order]

        def softplus(h):
            hl[h] = _sb_softplus(zb[h], vis_blocks)

        def suffix(h):
            rt[h] = jnp.dot(hl[h], w2, preferred_element_type=F32)

        def accumulate(h):
            a_blocks, total = _sb_weights(zb[h], vis_blocks, rt[h], None)
            a = jnp.concatenate(a_blocks[::-1], axis=1).astype(BF16)
            p = jnp.dot(a, v_ref[r, heads[h]].astype(BF16), preferred_element_type=F32)
            run = run_scr[h]
            acc_scr[h] += p * jnp.exp2(-run)
            run_scr[h] = run + total

        _skewed(N_HEADS, [scores, softplus, suffix, accumulate])

    slab(i, True)

    def body(t, carry):
        slab(i - 1 - t, False)
        return carry
    lax.fori_loop(0, i, body, 0)

    for h in range(N_HEADS):
        c = slice(h * HEAD_DIM, (h + 1) * HEAD_DIM)
        o_ref[:, c] = _head_norm(acc_scr[h], gain_ref[:, c]).astype(o_ref.dtype)


def _sb_prompt(pq, q_col0, k, v, bias_rows, gain, n_seq, seq):
    nq = seq // SB_TILE
    qoff = q_col0 // D_GROUP
    kv = pl.BlockSpec((seq, D_GROUP), lambda b, i: (b, 0))
    return pl.pallas_call(
        _sbp_kernel,
        grid=(n_seq, nq),
        in_specs=[
            pl.BlockSpec((SB_TILE, D_GROUP), lambda b, i: (b * nq + i, qoff)),
            kv, kv,
            pl.BlockSpec((N_HEADS, 1, SB_TILE), lambda b, i: (0, 0, 0)),
            pl.BlockSpec((1, D_GROUP), lambda b, i: (0, 0)),
        ],
        out_specs=pl.BlockSpec((SB_TILE, D_GROUP), lambda b, i: (b * nq + i, 0)),
        out_shape=jax.ShapeDtypeStruct((n_seq * seq, D_GROUP), BF16),
        scratch_shapes=[pltpu.VMEM((SB_TILE, D_GROUP), BF16),
                        pltpu.VMEM((N_HEADS, SB_TILE, HEAD_DIM), F32),
                        pltpu.VMEM((N_HEADS, SB_TILE, LANE), F32)],
        compiler_params=_params(("arbitrary", "arbitrary")),
        name="stickbreak_prompt",
    )(pq, k, v, bias_rows, gain.reshape(1, D_GROUP))


def _sbs_kernel(pt_ref, q_ref, kn_ref, vn_ref, *rest, n_new):
    del pt_ref
    kc_refs = rest[:SB_PAGES]
    vc_refs = rest[SB_PAGES:2 * SB_PAGES]
    bias_ref, gain_ref, o_ref, q_scr, acc_scr, run_scr = rest[2 * SB_PAGES:]
    j = pl.program_id(1)
    nr = N_HEADS * n_new
    w2 = _suffix_weights()
    row = lax.broadcasted_iota(jnp.int32, (nr, LANE), 0)
    col = lax.broadcasted_iota(jnp.int32, (nr, LANE), 1)
    own = (col & (N_HEADS - 1)) == (row >> (n_new.bit_length() - 1))

    def visit(k_rows, v_rows, vis):
        n = len(k_rows)
        q = q_scr[...]
        zb, hl, rt, a = [None] * n, [None] * n, [None] * n, [None] * n
        state = {"run": run_scr[...], "acc": acc_scr[...]}

        def scores(p):
            z = lax.dot_general(q, k_rows[p](), NT_DIMS, preferred_element_type=F32)
            zb[p] = [z[:, m * LANE:(m + 1) * LANE] + bias_ref[...]
                     for m in range(z.shape[1] // LANE - 1, -1, -1)]

        def softplus(p):
            hl[p] = _sb_softplus(zb[p], [vis] * len(zb[p]))

        def suffix(p):
            rt[p] = jnp.dot(hl[p], w2, preferred_element_type=F32)

        def weights(p):
            blocks, state["run"] = _sb_weights(zb[p], [vis] * len(zb[p]), rt[p], state["run"])
            blocks = blocks[::-1]
            a[p] = (blocks[0] if len(blocks) == 1
                    else jnp.concatenate(blocks, axis=1)).astype(BF16)

        def accumulate(p):
            state["acc"] = state["acc"] + jnp.dot(a[p], v_rows[p](), preferred_element_type=F32)

        _skewed(n, [scores, softplus, suffix, weights, accumulate])
        acc_scr[...] = state["acc"]
        run_scr[...] = state["run"]

    @pl.when(j == 0)
    def _():
        q = q_ref[...] * (ATT_SCALE * LOG2E)
        q_scr[...] = jnp.concatenate(
            [q[:, h * HEAD_DIM:(h + 1) * HEAD_DIM] for h in range(N_HEADS)], axis=0).astype(BF16)
        run_scr[...] = jnp.zeros_like(run_scr)
        acc_scr[...] = jnp.zeros_like(acc_scr)
        pad = jnp.zeros((LANE - nr, HEAD_DIM), F32)
        kb = jnp.concatenate([kn_ref[...], pad], axis=0).astype(BF16)
        vb = jnp.concatenate([vn_ref[...], pad], axis=0).astype(BF16)
        earlier = (col >> (N_HEADS.bit_length() - 1)) < (row & (n_new - 1))
        visit([lambda: kb], [lambda: vb], own & earlier)

    def page_rows(ref):
        return lambda: ref[...].reshape(PAGE * N_HEADS, HEAD_DIM).astype(BF16)
    visit([page_rows(r) for r in kc_refs], [page_rows(r) for r in vc_refs], own)

    @pl.when(j == pl.num_programs(1) - 1)
    def _():
        acc = acc_scr[...]
        heads = [acc[h * n_new:(h + 1) * n_new] for h in range(N_HEADS)]
        heads = [o * lax.rsqrt(jnp.mean(o * o, axis=-1, keepdims=True) + EPS) for o in heads]
        o_ref[...] = jnp.concatenate(heads, axis=1) * gain_ref[...]


def _sb_sample(pq, q_col0, k_new, v_new, cache_k, cache_v, page_table, bias_rep, gain,
               n_seq, n_new):
    n_pages = page_table.shape[1]
    qoff = q_col0 // D_GROUP
    nr = N_HEADS * n_new
    assert nr <= LANE
    new = pl.BlockSpec((None, nr, HEAD_DIM), lambda b, j, pt: (b, 0, 0))

    def page(r):
        return pl.BlockSpec(
            (None, None, PAGE, N_HEADS, HEAD_DIM),
            lambda b, j, pt: (0, pt[b, n_pages - 1 - (j * SB_PAGES + r)], 0, 0, 0))
    pages = [page(r) for r in range(SB_PAGES)]
    grid_spec = pltpu.PrefetchScalarGridSpec(
        num_scalar_prefetch=1,
        grid=(n_seq, n_pages // SB_PAGES),
        in_specs=[pl.BlockSpec((n_new, D_GROUP), lambda b, j, pt: (b, qoff)), new, new,
                  *pages, *pages,
                  pl.BlockSpec((nr, LANE), lambda b, j, pt: (0, 0)),
                  pl.BlockSpec((1, D_GROUP), lambda b, j, pt: (0, 0))],
        out_specs=pl.BlockSpec((n_new, D_GROUP), lambda b, j, pt: (b, 0)),
        scratch_shapes=[pltpu.VMEM((nr, HEAD_DIM), BF16), pltpu.VMEM((nr, HEAD_DIM), F32),
                        pltpu.VMEM((nr, LANE), F32)],
    )
    return pl.pallas_call(
        functools.partial(_sbs_kernel, n_new=n_new),
        grid_spec=grid_spec,
        out_shape=jax.ShapeDtypeStruct((n_seq * n_new, D_GROUP), F32),
        compiler_params=_params(("arbitrary", "arbitrary")),
        name="stickbreak_sample",
    )(page_table, pq, k_new, v_new, *([cache_k] * SB_PAGES), *([cache_v] * SB_PAGES),
      bias_rep, gain.reshape(1, D_GROUP))


def kernel(x_prompt, x_sample, cache_k, cache_v, state_hgrn, page_table, c_prompt, c_sample,
           lb_logits, norm_ffn1, norm_mix, norm_ffn2, w_mod, b_mod,
           w_ffn1_gate, w_ffn1_up, w_ffn1_down, w_in, g_out_a, g_out_b, b_sb, w_out,
           w_ffn2_gate, w_ffn2_up, w_ffn2_down, norm_final, w_final_mod, b_final_mod):
    n_p, seq, d = x_prompt.shape
    n_s, n_new, _ = x_sample.shape
    assert w_mod.shape[0] == 1, "single-layer trunk"
    assert n_p + n_s <= MOD_ROWS
    assert n_new & (n_new - 1) == 0 and n_new <= PAGE
    assert page_table.shape[1] % SB_PAGES == 0 and seq % SB_TILE == 0

    c_rows = jnp.concatenate(
        [c_prompt, c_sample, jnp.zeros((MOD_ROWS - n_p - n_s, d), F32)], axis=0)
    mod = _modulation(c_rows, w_mod[0], b_mod[0], 1024).reshape(MOD_ROWS, N_MOD, d)
    fmod = _modulation(c_rows, w_final_mod, b_final_mod, 1024).reshape(MOD_ROWS, 2, d)

    bf = lambda w: w[0].astype(BF16)
    wg1, wu1, wd1 = bf(w_ffn1_gate), bf(w_ffn1_up), bf(w_ffn1_down)
    wg2, wu2, wd2 = bf(w_ffn2_gate), bf(w_ffn2_up), bf(w_ffn2_down)
    w_in_b, w_out_b = bf(w_in), bf(w_out)
    bias2 = b_sb[0] * LOG2E
    bias_rows = jnp.broadcast_to(bias2.reshape(N_HEADS, 1, 1), (N_HEADS, 1, SB_TILE))
    bias_rep = jnp.broadcast_to(jnp.repeat(bias2, n_new)[:, None], (N_HEADS * n_new, LANE))

    def layer(x, mods, fmods, mod_spec, mod_spec1, tm, tf, tm_proj, mixer):
        x1, h2 = _ffn(x, norm_ffn1[0], mods[0:3], wg1, wu1, wd1, norm_mix[0], mods[3:5],
                      mod_spec, tm, tf, final=False)
        pa = _in_proj(h2, w_in_b, 0, 5 * D_GROUP, tm_proj, 1024)
        k_new = _in_proj(h2, w_in_b, 5 * D_GROUP, D_GROUP, tm_proj, 1024)
        v_new = _in_proj(h2, w_in_b, 6 * D_GROUP, D_GROUP, tm_proj, 1024)
        oa, ob, s_fin = mixer(pa, k_new, v_new)
        x2 = _out_proj(oa, ob, w_out_b, x1, mods[5], mod_spec1, tm)
        y = _ffn(x2, norm_ffn2[0], mods[6:9], wg2, wu2, wd2, norm_final, fmods,
                 mod_spec, tm, tf, final=True)
        return y, k_new, v_new, s_fin

    tm_p = 512
    tiles_per_seq = seq // tm_p
    mods_p = [mod[:n_p, j].reshape(n_p, 1, d) for j in range(N_MOD)]
    fmods_p = [fmod[:n_p, j].reshape(n_p, 1, d) for j in range(2)]
    spec_p = pl.BlockSpec((None, 1, d), lambda i, f: (i // tiles_per_seq, 0, 0))
    spec_p1 = pl.BlockSpec((None, 1, d), lambda i: (i // tiles_per_seq, 0, 0))

    def mixer_p(pa, k_new, v_new):
        s0 = jnp.zeros((n_p, N_HEADS, HEAD_DIM, HEAD_DIM), F32)
        oa, s_fin = _gla(pa, lb_logits[0], g_out_a[0], s0, n_p, seq, GLA_CHUNK, LANE, BF16)
        ob = _sb_prompt(pa, 4 * D_GROUP, k_new, v_new, bias_rows, g_out_b[0], n_p, seq)
        return oa, ob, s_fin

    y_p, k_p, v_p, s_p = layer(x_prompt.reshape(n_p * seq, d), mods_p, fmods_p,
                               spec_p, spec_p1, tm_p, 512, 1024, mixer_p)

    rows_s = n_s * n_new
    mods_s = [jnp.repeat(mod[n_p:n_p + n_s, j], n_new, axis=0).reshape(1, rows_s, d)
              for j in range(N_MOD)]
    fmods_s = [jnp.repeat(fmod[n_p:n_p + n_s, j], n_new, axis=0).reshape(1, rows_s, d)
               for j in range(2)]
    spec_s = pl.BlockSpec((None, rows_s, d), lambda i, f: (0, 0, 0))
    spec_s1 = pl.BlockSpec((None, rows_s, d), lambda i: (0, 0, 0))

    def mixer_s(pa, k_new, v_new):
        pa_pad = jnp.pad(pa[:, :4 * D_GROUP].reshape(n_s, n_new, 4 * D_GROUP),
                         ((0, 0), (0, LANE - n_new), (0, 0))).reshape(n_s * LANE, 4 * D_GROUP)
        oa_pad, s_fin = _gla(pa_pad, lb_logits[0], g_out_a[0], state_hgrn[0], n_s, LANE, LANE,
                             n_new, F32)
        oa = oa_pad.reshape(n_s, LANE, D_GROUP)[:, :n_new].reshape(rows_s, D_GROUP)
        rows_th = (n_s, n_new * N_HEADS, HEAD_DIM)
        ob = _sb_sample(pa, 4 * D_GROUP, k_new.reshape(rows_th), v_new.reshape(rows_th),
                        cache_k, cache_v, page_table, bias_rep, g_out_b[0], n_s, n_new)
        return oa, ob, s_fin

    y_s, k_s, v_s, s_s = layer(x_sample.reshape(rows_s, d), mods_s, fmods_s,
                               spec_s, spec_s1, rows_s, 1408, rows_s, mixer_s)

    return (y_p.reshape(n_p, seq, d), y_s.reshape(n_s, n_new, d),
            k_p.reshape(1, n_p, seq, N_HEADS, HEAD_DIM), v_p.reshape(1, n_p, seq, N_HEADS, HEAD_DIM),
            k_s.reshape(1, n_s, n_new, N_HEADS, HEAD_DIM), v_s.reshape(1, n_s, n_new, N_HEADS, HEAD_DIM),
            s_p[None], s_s[None])
```

```python
import functools

import jax
import jax.numpy as jnp
from jax import lax
from jax.experimental import pallas as pl
from jax.experimental.pallas import tpu as pltpu

F32 = jnp.float32
BF16 = jnp.bfloat16

N_HEADS = 8
HEAD_DIM = 128
D_GROUP = N_HEADS * HEAD_DIM
N_MOD = 9
GLA_CHUNK = 32
GLA_GROUPS = 4
CAST_SLABS = 32
PAGE = 128
EPS = 1e-6
ATT_SCALE = HEAD_DIM ** -0.5
LOG2E = 1.4426950408889634
MASKED = -1e30
MOD_ROWS = 16
LANE = 128
SB_TILE = 256
SB_PAGES = 8
VMEM_LIMIT = 56 * 1024 * 1024

NT_DIMS = (((1,), (1,)), ((), ()))


def _params(sem):
    return pltpu.CompilerParams(dimension_semantics=sem, vmem_limit_bytes=VMEM_LIMIT)


def _sigmoid(x):
    return 1.0 / (1.0 + jnp.exp(-x))


def _silu(x):
    return x * _sigmoid(x)


def _adaln(x, nw, shift, scale):
    ms = jnp.mean(x * x, axis=-1, keepdims=True)
    return x * lax.rsqrt(ms + EPS) * (nw * (1.0 + scale)) + shift


def _head_norm(o, gain):
    ms = jnp.mean(o * o, axis=-1, keepdims=True)
    return o * lax.rsqrt(ms + EPS) * gain


def _split2(x):
    hi = x.astype(BF16)
    lo = (x - hi.astype(F32)).astype(BF16)
    return hi, lo


def _mod_kernel(c_ref, w_ref, b_ref, o_ref):
    a = _silu(c_ref[...]).astype(BF16)
    o_ref[...] = jnp.dot(a, w_ref[...].astype(BF16), preferred_element_type=F32) + b_ref[...]


def _modulation(c_rows, w, b, tn):
    d, n = w.shape
    return pl.pallas_call(
        _mod_kernel,
        grid=(n // tn,),
        in_specs=[
            pl.BlockSpec((MOD_ROWS, d), lambda j: (0, 0)),
            pl.BlockSpec((d, tn), lambda j: (0, j)),
            pl.BlockSpec((1, tn), lambda j: (0, j)),
        ],
        out_specs=pl.BlockSpec((MOD_ROWS, tn), lambda j: (0, j)),
        out_shape=jax.ShapeDtypeStruct((MOD_ROWS, n), F32),
        compiler_params=_params(("arbitrary",)),
        name="modulation",
    )(c_rows, w, b.reshape(1, n))


def _ffn_kernel(x_ref, nw_ref, sh_ref, sc_ref, ga_ref, wg_ref, wu_ref, wd_ref,
                nw2_ref, sh2_ref, sc2_ref, *rest, final):
    if final:
        y_ref, h_scr, acc_scr = rest
    else:
        xo_ref, y_ref, h_scr, acc_scr = rest
    f = pl.program_id(1)

    @pl.when(f == 0)
    def _():
        h = _adaln(x_ref[...], nw_ref[...], sh_ref[...], sc_ref[...])
        h_scr[...] = h.astype(BF16)
        acc_scr[...] = jnp.zeros_like(acc_scr)

    h = h_scr[...]
    g = jnp.dot(h, wg_ref[...], preferred_element_type=F32)
    u = jnp.dot(h, wu_ref[...], preferred_element_type=F32)
    a = (_silu(g) * u).astype(BF16)
    acc_scr[...] += jnp.dot(a, wd_ref[...], preferred_element_type=F32)

    @pl.when(f == pl.num_programs(1) - 1)
    def _():
        xn = x_ref[...] + 0.5 * ga_ref[...] * acc_scr[...]
        y = _adaln(xn, nw2_ref[...], sh2_ref[...], sc2_ref[...])
        if not final:
            xo_ref[...] = xn
        y_ref[...] = y.astype(y_ref.dtype)


def _ffn(x, nw, mods, wg, wu, wd, nw2, mods2, mod_spec, tm, tf, final):
    n, d = x.shape
    nf = wg.shape[1]
    row = pl.BlockSpec((tm, d), lambda i, f: (i, 0))
    vec = pl.BlockSpec((1, d), lambda i, f: (0, 0))
    in_specs = [row, vec, mod_spec, mod_spec, mod_spec,
                pl.BlockSpec((d, tf), lambda i, f: (0, f)),
                pl.BlockSpec((d, tf), lambda i, f: (0, f)),
                pl.BlockSpec((tf, d), lambda i, f: (f, 0)),
                vec, mod_spec, mod_spec]
    if final:
        out_specs = row
        out_shape = jax.ShapeDtypeStruct((n, d), F32)
    else:
        out_specs = (row, row)
        out_shape = (jax.ShapeDtypeStruct((n, d), F32), jax.ShapeDtypeStruct((n, d), BF16))
    return pl.pallas_call(
        functools.partial(_ffn_kernel, final=final),
        grid=(n // tm, nf // tf),
        in_specs=in_specs,
        out_specs=out_specs,
        out_shape=out_shape,
        scratch_shapes=[pltpu.VMEM((tm, d), BF16), pltpu.VMEM((tm, d), F32)],
        compiler_params=_params(("arbitrary", "arbitrary")),
        name="ffn_final" if final else "ffn",
    )(x, nw.reshape(1, d), mods[0], mods[1], mods[2], wg, wu, wd,
      nw2.reshape(1, d), mods2[0], mods2[1])


def _in_proj_kernel(h_ref, w_ref, *rest, n_cast):
    src, o_ref, dst, wb_scr = rest[:n_cast], rest[n_cast], rest[n_cast + 1:-1], rest[-1]

    @pl.when(pl.program_id(1) == 0)
    def _():
        wb_scr[...] = w_ref[...].astype(BF16)

    o_ref[...] = jnp.dot(h_ref[...], wb_scr[...], preferred_element_type=F32)
    for s_ref, d_ref in zip(src, dst):
        d_ref[...] = s_ref[...].astype(BF16)


def _in_proj(h, w, col0, ncols, tm, tn, cast=()):
    n, d = h.shape
    off = col0 // tn
    grid = (ncols // tn, n // tm)

    def slab(a):
        rows = a.shape[0] // CAST_SLABS
        assert a.shape[0] % CAST_SLABS == 0 and rows % 16 == 0
        return pl.BlockSpec(
            (rows, a.shape[1]), lambda j, i: (jnp.minimum(j * grid[1] + i, CAST_SLABS - 1), 0))
    assert not cast or grid[0] * grid[1] >= CAST_SLABS
    slabs = [slab(a) for a in cast]
    return pl.pallas_call(
        functools.partial(_in_proj_kernel, n_cast=len(cast)),
        grid=grid,
        in_specs=[
            pl.BlockSpec((tm, d), lambda j, i: (i, 0)),
            pl.BlockSpec((d, tn), lambda j, i: (0, j + off)),
            *slabs,
        ],
        out_specs=[pl.BlockSpec((tm, tn), lambda j, i: (i, j)), *slabs],
        out_shape=[jax.ShapeDtypeStruct((n, ncols), F32),
                   *[jax.ShapeDtypeStruct(a.shape, BF16) for a in cast]],
        scratch_shapes=[pltpu.VMEM((d, tn), BF16)],
        compiler_params=_params(("arbitrary", "arbitrary")),
        name="in_proj",
    )(h, w, *cast)


def _out_proj_kernel(oa_ref, ob_ref, w_ref, x_ref, ga_ref, o_ref):
    m = jnp.dot(oa_ref[...].astype(BF16), w_ref[:D_GROUP, :], preferred_element_type=F32)
    m += jnp.dot(ob_ref[...].astype(BF16), w_ref[D_GROUP:, :], preferred_element_type=F32)
    o_ref[...] = x_ref[...] + ga_ref[...] * m


def _out_proj(oa, ob, w, x, gate, mod_spec2, tm):
    n, d = x.shape
    half = pl.BlockSpec((tm, D_GROUP), lambda i: (i, 0))
    row = pl.BlockSpec((tm, d), lambda i: (i, 0))
    return pl.pallas_call(
        _out_proj_kernel,
        grid=(n // tm,),
        in_specs=[half, half, pl.BlockSpec((2 * D_GROUP, d), lambda i: (0, 0)), row, mod_spec2],
        out_specs=row,
        out_shape=jax.ShapeDtypeStruct((n, d), F32),
        compiler_params=_params(("arbitrary",)),
        name="out_proj",
    )(oa, ob, w, x, gate)


def _gla_kernel(q_ref, f_ref, i_ref, g_ref, lbl_ref, gain_ref, s0_ref, o_ref, s_ref, st_scr,
                *, chunk, n_valid, n_groups):
    t = pl.program_id(1)
    n_chunks = LANE // chunk
    shift = chunk.bit_length() - 1

    @pl.when(t == 0)
    def _():
        for h in range(N_HEADS):
            st_scr[h] = s0_ref[h].T

    r_id = lax.broadcasted_iota(jnp.int32, (LANE, LANE), 0)
    c_id = lax.broadcasted_iota(jnp.int32, (LANE, LANE), 1)
    same = (r_id >> shift) == (c_id >> shift)
    causal = same & (c_id <= r_id)
    m_cum = jnp.concatenate([jnp.where(causal, 1.0, 0.0), jnp.where(same, 1.0, 0.0)],
                            axis=0).astype(BF16)

    heads = [slice(h * HEAD_DIM, (h + 1) * HEAD_DIM) for h in range(N_HEADS)]
    lbl = lbl_ref[...]
    mx = jnp.maximum(lbl, 0.0)
    e1 = jnp.exp(lbl - mx)
    lb = e1 / (e1 + jnp.exp(-mx))

    rows = [slice(g * LANE, (g + 1) * LANE) for g in range(n_groups)]
    kk, bb, v, vb, qd, k_inv, k_end, decay, o, v_t = ([None] * n_groups for _ in range(10))
    st = [st_scr[h] for h in range(N_HEADS)]

    def gates(g):
        f = lb + (1.0 - lb) * _sigmoid(f_ref[rows[g], :])
        logf = jnp.log(f)
        k = 1.0 - f
        if n_valid < LANE:
            valid = lax.broadcasted_iota(jnp.int32, (LANE, D_GROUP), 0) < n_valid
            logf = jnp.where(valid, logf, 0.0)
            k = jnp.where(valid, k, 0.0)
        hi = logf.astype(BF16)
        r1 = logf - hi.astype(F32)
        mid = r1.astype(BF16)
        lo = (r1 - mid.astype(F32)).astype(BF16)
        kk[g] = k
        bb[g] = jnp.dot(m_cum, jnp.concatenate([hi, mid, lo], axis=1),
                        preferred_element_type=F32)

    def decays(g):
        s = bb[g][:, :D_GROUP] + bb[g][:, D_GROUP:2 * D_GROUP] + bb[g][:, 2 * D_GROUP:]
        b = s[:LANE]
        b_last = s[LANE:]
        v[g] = i_ref[rows[g], :]
        vb[g] = v[g].astype(BF16)
        qd[g] = (q_ref[rows[g], :] * ATT_SCALE * jnp.exp(b)).astype(BF16)
        k_inv[g] = (kk[g] * jnp.exp(-b)).astype(BF16)
        k_end[g] = (kk[g] * jnp.exp(b_last - b)).astype(BF16)
        decay[g] = jnp.exp(b_last)

    def intra(g):
        a = [lax.dot_general(qd[g][:, c], k_inv[g][:, c], NT_DIMS, preferred_element_type=F32)
             for c in heads]
        a = [jnp.where(causal, x, 0.0).astype(BF16) for x in a]
        o[g] = [jnp.dot(x, vb[g][:, c], preferred_element_type=F32) for x, c in zip(a, heads)]
        v_t[g] = [v[g][:, c].T for c in heads]

    def state(g):
        o_state = [[] for _ in heads]
        for j in range(n_chunks):
            for h, c in enumerate(heads):
                o_state[h].append(lax.dot_general(qd[g][j * chunk:(j + 1) * chunk, c],
                                                  st[h].astype(BF16), NT_DIMS,
                                                  preferred_element_type=F32))
                if n_chunks > 1:
                    v_tj = jnp.where((c_id >> shift) == j, v_t[g][h], 0.0).astype(BF16)
                else:
                    v_tj = v_t[g][h].astype(BF16)
                upd = jnp.dot(v_tj, k_end[g][:, c], preferred_element_type=F32)
                st[h] = decay[g][j * chunk:j * chunk + 1, c] * st[h] + upd
        for h in range(N_HEADS):
            o[g][h] = o[g][h] + (jnp.concatenate(o_state[h], axis=0) if n_chunks > 1
                                 else o_state[h][0])

    def emit(g):
        for h, c in enumerate(heads):
            o_ref[rows[g], c] = (_head_norm(o[g][h], gain_ref[:, c])
                                 * _silu(g_ref[rows[g], c])).astype(o_ref.dtype)

    _skewed(n_groups, [gates, decays, intra, state, emit])
    for h in range(N_HEADS):
        st_scr[h] = st[h]

    @pl.when(t == pl.num_programs(1) - 1)
    def _():
        for h in range(N_HEADS):
            s_ref[h] = st_scr[h].T


def _gla(pa, lb_logits, gain, s0, n_seq, rows, chunk, n_valid, out_dtype):
    n_groups = min(GLA_GROUPS, rows // LANE)
    tile = n_groups * LANE
    n_tiles = rows // tile

    def col(group):
        return pl.BlockSpec((tile, D_GROUP), lambda b, t: (b * n_tiles + t, group))
    vec = pl.BlockSpec((1, D_GROUP), lambda b, t: (0, 0))
    state = pl.BlockSpec((None, N_HEADS, HEAD_DIM, HEAD_DIM), lambda b, t: (b, 0, 0, 0))
    return pl.pallas_call(
        functools.partial(_gla_kernel, chunk=chunk, n_valid=n_valid, n_groups=n_groups),
        grid=(n_seq, n_tiles),
        in_specs=[col(0), col(1), col(2), col(3), vec, vec, state],
        out_specs=(pl.BlockSpec((tile, D_GROUP), lambda b, t: (b * n_tiles + t, 0)), state),
        out_shape=(jax.ShapeDtypeStruct((n_seq * rows, D_GROUP), out_dtype),
                   jax.ShapeDtypeStruct((n_seq, N_HEADS, HEAD_DIM, HEAD_DIM), F32)),
        scratch_shapes=[pltpu.VMEM((N_HEADS, HEAD_DIM, HEAD_DIM), F32)],
        compiler_params=_params(("arbitrary", "arbitrary")),
        name="hgrn2",
    )(pa, pa, pa, pa, lb_logits.reshape(1, D_GROUP), gain.reshape(1, D_GROUP), s0)


def _suffix_weights():
    r_id = lax.broadcasted_iota(jnp.int32, (2 * LANE, 2 * LANE), 0) & (LANE - 1)
    c_id = lax.broadcasted_iota(jnp.int32, (2 * LANE, 2 * LANE), 1)
    return jnp.where((r_id >= c_id) | (c_id >= LANE), 1.0, 0.0).astype(BF16)


def _softplus2(z):
    return jnp.maximum(z, 0.0) + jnp.log(1.0 + jnp.exp2(-jnp.abs(z))) * LOG2E


def _sb_softplus(z_blocks):
    sp = [_softplus2(z) for z in z_blocks]
    hi, lo = _split2(sp[0] if len(sp) == 1 else jnp.concatenate(sp, axis=0))
    return jnp.concatenate([hi, lo], axis=1)


def _sb_weights(z_blocks, rt, run):
    m = z_blocks[0].shape[0]
    out = []
    for n, z in enumerate(z_blocks):
        within = rt[n * m:(n + 1) * m, :LANE]
        total = rt[n * m:(n + 1) * m, LANE:]
        out.append(jnp.exp2(z - within if run is None else z - within - run))
        run = total if run is None else run + total
    return out, run


def _skewed(n_items, stages):
    for tick in range(n_items + len(stages) - 1):
        for s in range(len(stages) - 1, -1, -1):
            i = tick - s
            if 0 <= i < n_items:
                stages[s](i)


def _sbp_kernel(q_ref, k_ref, v_ref, bias_ref, gain_ref, o_ref, q_scr, acc_scr, run_scr):
    i = pl.program_id(1)
    w2 = _suffix_weights()
    q_scr[...] = (q_ref[...] * (ATT_SCALE * LOG2E)).astype(BF16)
    acc_scr[...] = jnp.zeros_like(acc_scr)
    run_scr[...] = jnp.zeros_like(run_scr)
    r_id = lax.broadcasted_iota(jnp.int32, (SB_TILE, LANE), 0)
    c_id = lax.broadcasted_iota(jnp.int32, (SB_TILE, LANE), 1)
    n_blk = SB_TILE // LANE

    def slab(j, diagonal):
        r = pl.ds(pl.multiple_of(j * SB_TILE, SB_TILE), SB_TILE)
        order = range(n_blk - 1, -1, -1)
        if diagonal:
            mask = [jnp.where(c_id + n * LANE < r_id, 0.0, MASKED) for n in order]
        heads = [slice(h * HEAD_DIM, (h + 1) * HEAD_DIM) for h in range(N_HEADS)]
        zb, hl, rt = [None] * N_HEADS, [None] * N_HEADS, [None] * N_HEADS

        def scores(h):
            kb = k_ref[r, heads[h]].astype(BF16)
            z = lax.dot_general(q_scr[:, heads[h]], kb, NT_DIMS,
                                preferred_element_type=F32) + bias_ref[h]
            zb[h] = [z[:, n * LANE:(n + 1) * LANE] for n in order]
            if diagonal:
                zb[h] = [z_n + m_n for z_n, m_n in zip(zb[h], mask)]

        def softplus(h):
            hl[h] = _sb_softplus(zb[h])

        def suffix(h):
            rt[h] = jnp.dot(hl[h], w2, preferred_element_type=F32)

        def accumulate(h):
            a_blocks, total = _sb_weights(zb[h], rt[h], None)
            a = jnp.concatenate(a_blocks[::-1], axis=1).astype(BF16)
            p = jnp.dot(a, v_ref[r, heads[h]].astype(BF16), preferred_element_type=F32)
            run = run_scr[h]
            acc_scr[h] += p * jnp.exp2(-run)
            run_scr[h] = run + total

        _skewed(N_HEADS, [scores, softplus, suffix, accumulate])

    slab(i, True)

    def body(t, carry):
        slab(i - 1 - t, False)
        return carry
    lax.fori_loop(0, i, body, 0)

    for h in range(N_HEADS):
        c = slice(h * HEAD_DIM, (h + 1) * HEAD_DIM)
        o_ref[:, c] = _head_norm(acc_scr[h], gain_ref[:, c]).astype(o_ref.dtype)


def _sb_prompt(pq, q_col0, k, v, bias_rows, gain, n_seq, seq):
    nq = seq // SB_TILE
    qoff = q_col0 // D_GROUP
    kv = pl.BlockSpec((seq, D_GROUP), lambda b, i: (b, 0))
    return pl.pallas_call(
        _sbp_kernel,
        grid=(n_seq, nq),
        in_specs=[
            pl.BlockSpec((SB_TILE, D_GROUP), lambda b, i: (b * nq + i, qoff)),
            kv, kv,
            pl.BlockSpec((N_HEADS, 1, SB_TILE), lambda b, i: (0, 0, 0)),
            pl.BlockSpec((1, D_GROUP), lambda b, i: (0, 0)),
        ],
        out_specs=pl.BlockSpec((SB_TILE, D_GROUP), lambda b, i: (b * nq + i, 0)),
        out_shape=jax.ShapeDtypeStruct((n_seq * seq, D_GROUP), BF16),
        scratch_shapes=[pltpu.VMEM((SB_TILE, D_GROUP), BF16),
                        pltpu.VMEM((N_HEADS, SB_TILE, HEAD_DIM), F32),
                        pltpu.VMEM((N_HEADS, SB_TILE, LANE), F32)],
        compiler_params=_params(("arbitrary", "arbitrary")),
        name="stickbreak_prompt",
    )(pq, k, v, bias_rows, gain.reshape(1, D_GROUP))


def _sbs_kernel(pt_ref, q_ref, kn_ref, vn_ref, *rest, n_new):
    del pt_ref
    kc_refs = rest[:SB_PAGES]
    vc_refs = rest[SB_PAGES:2 * SB_PAGES]
    bias_ref, gain_ref, o_ref, q_scr, acc_scr, run_scr = rest[2 * SB_PAGES:]
    j = pl.program_id(1)
    nr = N_HEADS * n_new
    w2 = _suffix_weights()
    row = lax.broadcasted_iota(jnp.int32, (nr, LANE), 0)
    col = lax.broadcasted_iota(jnp.int32, (nr, LANE), 1)
    own = (col & (N_HEADS - 1)) == (row >> (n_new.bit_length() - 1))

    def visit(k_rows, v_rows, bias):
        n = len(k_rows)
        q = q_scr[...]
        zb, hl, rt, a = [None] * n, [None] * n, [None] * n, [None] * n
        state = {"run": run_scr[...], "acc": acc_scr[...]}

        def scores(p):
            z = lax.dot_general(q, k_rows[p](), NT_DIMS, preferred_element_type=F32)
            zb[p] = [z[:, m * LANE:(m + 1) * LANE] + bias
                     for m in range(z.shape[1] // LANE - 1, -1, -1)]

        def softplus(p):
            hl[p] = _sb_softplus(zb[p])

        def suffix(p):
            rt[p] = jnp.dot(hl[p], w2, preferred_element_type=F32)

        def weights(p):
            blocks, state["run"] = _sb_weights(zb[p], rt[p], state["run"])
            blocks = blocks[::-1]
            a[p] = (blocks[0] if len(blocks) == 1
                    else jnp.concatenate(blocks, axis=1)).astype(BF16)

        def accumulate(p):
            state["acc"] = state["acc"] + jnp.dot(a[p], v_rows[p](), preferred_element_type=F32)

        _skewed(n, [scores, softplus, suffix, weights, accumulate])
        acc_scr[...] = state["acc"]
        run_scr[...] = state["run"]

    @pl.when(j == 0)
    def _():
        q = q_ref[...] * (ATT_SCALE * LOG2E)
        q_scr[...] = jnp.concatenate(
            [q[:, h * HEAD_DIM:(h + 1) * HEAD_DIM] for h in range(N_HEADS)], axis=0).astype(BF16)
        run_scr[...] = jnp.zeros_like(run_scr)
        acc_scr[...] = jnp.zeros_like(acc_scr)
        pad = jnp.zeros((LANE - nr, HEAD_DIM), F32)
        kb = jnp.concatenate([kn_ref[...], pad], axis=0).astype(BF16)
        vb = jnp.concatenate([vn_ref[...], pad], axis=0).astype(BF16)
        earlier = (col >> (N_HEADS.bit_length() - 1)) < (row & (n_new - 1))
        visit([lambda: kb], [lambda: vb], jnp.where(own & earlier, bias_ref[...], MASKED))

    def page_rows(ref):
        return lambda: ref[...].reshape(PAGE * N_HEADS, HEAD_DIM).astype(BF16)
    visit([page_rows(r) for r in kc_refs], [page_rows(r) for r in vc_refs],
          jnp.where(own, bias_ref[...], MASKED))

    @pl.when(j == pl.num_programs(1) - 1)
    def _():
        acc = acc_scr[...]
        heads = [acc[h * n_new:(h + 1) * n_new] for h in range(N_HEADS)]
        heads = [o * lax.rsqrt(jnp.mean(o * o, axis=-1, keepdims=True) + EPS) for o in heads]
        o_ref[...] = jnp.concatenate(heads, axis=1) * gain_ref[...]


def _sb_sample(pq, q_col0, k_new, v_new, cache_k, cache_v, page_table, bias_rep, gain,
               n_seq, n_new):
    n_pages = page_table.shape[1]
    qoff = q_col0 // D_GROUP
    nr = N_HEADS * n_new
    assert nr <= LANE
    new = pl.BlockSpec((None, nr, HEAD_DIM), lambda b, j, pt: (b, 0, 0))

    def page(r):
        return pl.BlockSpec(
            (None, None, PAGE, N_HEADS, HEAD_DIM),
            lambda b, j, pt: (0, pt[b, n_pages - 1 - (j * SB_PAGES + r)], 0, 0, 0))
    pages = [page(r) for r in range(SB_PAGES)]
    grid_spec = pltpu.PrefetchScalarGridSpec(
        num_scalar_prefetch=1,
        grid=(n_seq, n_pages // SB_PAGES),
        in_specs=[pl.BlockSpec((n_new, D_GROUP), lambda b, j, pt: (b, qoff)), new, new,
                  *pages, *pages,
                  pl.BlockSpec((nr, LANE), lambda b, j, pt: (0, 0)),
                  pl.BlockSpec((1, D_GROUP), lambda b, j, pt: (0, 0))],
        out_specs=pl.BlockSpec((n_new, D_GROUP), lambda b, j, pt: (b, 0)),
        scratch_shapes=[pltpu.VMEM((nr, HEAD_DIM), BF16), pltpu.VMEM((nr, HEAD_DIM), F32),
                        pltpu.VMEM((nr, LANE), F32)],
    )
    return pl.pallas_call(
        functools.partial(_sbs_kernel, n_new=n_new),
        grid_spec=grid_spec,
        out_shape=jax.ShapeDtypeStruct((n_seq * n_new, D_GROUP), F32),
        compiler_params=_params(("arbitrary", "arbitrary")),
        name="stickbreak_sample",
    )(page_table, pq, k_new, v_new, *([cache_k] * SB_PAGES), *([cache_v] * SB_PAGES),
      bias_rep, gain.reshape(1, D_GROUP))


def kernel(x_prompt, x_sample, cache_k, cache_v, state_hgrn, page_table, c_prompt, c_sample,
           lb_logits, norm_ffn1, norm_mix, norm_ffn2, w_mod, b_mod,
           w_ffn1_gate, w_ffn1_up, w_ffn1_down, w_in, g_out_a, g_out_b, b_sb, w_out,
           w_ffn2_gate, w_ffn2_up, w_ffn2_down, norm_final, w_final_mod, b_final_mod):
    n_p, seq, d = x_prompt.shape
    n_s, n_new, _ = x_sample.shape
    assert w_mod.shape[0] == 1, "single-layer trunk"
    assert n_p + n_s <= MOD_ROWS
    assert n_new & (n_new - 1) == 0 and n_new <= PAGE
    assert page_table.shape[1] % SB_PAGES == 0 and seq % SB_TILE == 0

    c_rows = jnp.concatenate(
        [c_prompt, c_sample, jnp.zeros((MOD_ROWS - n_p - n_s, d), F32)], axis=0)
    mod = _modulation(c_rows, w_mod[0], b_mod[0], 1024).reshape(MOD_ROWS, N_MOD, d)
    fmod = _modulation(c_rows, w_final_mod, b_final_mod, 1024).reshape(MOD_ROWS, 2, d)

    bf = lambda w: w[0].astype(BF16)
    wg1, wu1, wd1 = bf(w_ffn1_gate), bf(w_ffn1_up), bf(w_ffn1_down)
    w_out_b = bf(w_out)
    bias2 = b_sb[0] * LOG2E
    bias_rows = jnp.broadcast_to(bias2.reshape(N_HEADS, 1, 1), (N_HEADS, 1, SB_TILE))
    bias_rep = jnp.broadcast_to(jnp.repeat(bias2, n_new)[:, None], (N_HEADS * n_new, LANE))

    def layer(x, mods, fmods, mod_spec, mod_spec1, tm, tf, tm_proj, mixer, w_ffn2):
        x1, h2 = _ffn(x, norm_ffn1[0], mods[0:3], wg1, wu1, wd1, norm_mix[0], mods[3:5],
                      mod_spec, tm, tf, final=False)
        if w_ffn2 is None:
            pa, *w_ffn2 = _in_proj(h2, w_in[0], 0, 5 * D_GROUP, tm_proj, 1024,
                                   cast=(w_ffn2_gate[0], w_ffn2_up[0], w_ffn2_down[0]))
        else:
            pa, = _in_proj(h2, w_in[0], 0, 5 * D_GROUP, tm_proj, 1024)
        k_new, = _in_proj(h2, w_in[0], 5 * D_GROUP, D_GROUP, tm_proj, 1024)
        v_new, = _in_proj(h2, w_in[0], 6 * D_GROUP, D_GROUP, tm_proj, 1024)
        oa, ob, s_fin = mixer(pa, k_new, v_new)
        x2 = _out_proj(oa, ob, w_out_b, x1, mods[5], mod_spec1, tm)
        y = _ffn(x2, norm_ffn2[0], mods[6:9], *w_ffn2, norm_final, fmods,
                 mod_spec, tm, tf, final=True)
        return y, k_new, v_new, s_fin, w_ffn2

    tm_p = 512
    tiles_per_seq = seq // tm_p
    mods_p = [mod[:n_p, j].reshape(n_p, 1, d) for j in range(N_MOD)]
    fmods_p = [fmod[:n_p, j].reshape(n_p, 1, d) for j in range(2)]
    spec_p = pl.BlockSpec((None, 1, d), lambda i, f: (i // tiles_per_seq, 0, 0))
    spec_p1 = pl.BlockSpec((None, 1, d), lambda i: (i // tiles_per_seq, 0, 0))

    def mixer_p(pa, k_new, v_new):
        s0 = jnp.zeros((n_p, N_HEADS, HEAD_DIM, HEAD_DIM), F32)
        oa, s_fin = _gla(pa, lb_logits[0], g_out_a[0], s0, n_p, seq, GLA_CHUNK, LANE, BF16)
        ob = _sb_prompt(pa, 4 * D_GROUP, k_new, v_new, bias_rows, g_out_b[0], n_p, seq)
        return oa, ob, s_fin

    y_p, k_p, v_p, s_p, w_ffn2 = layer(x_prompt.reshape(n_p * seq, d), mods_p, fmods_p,
                                       spec_p, spec_p1, tm_p, 512, 1024, mixer_p, None)

    rows_s = n_s * n_new
    mods_s = [jnp.repeat(mod[n_p:n_p + n_s, j], n_new, axis=0).reshape(1, rows_s, d)
              for j in range(N_MOD)]
    fmods_s = [jnp.repeat(fmod[n_p:n_p + n_s, j], n_new, axis=0).reshape(1, rows_s, d)
               for j in range(2)]
    spec_s = pl.BlockSpec((None, rows_s, d), lambda i, f: (0, 0, 0))
    spec_s1 = pl.BlockSpec((None, rows_s, d), lambda i: (0, 0, 0))

    def mixer_s(pa, k_new, v_new):
        pa_pad = jnp.pad(pa[:, :4 * D_GROUP].reshape(n_s, n_new, 4 * D_GROUP),
                         ((0, 0), (0, LANE - n_new), (0, 0))).reshape(n_s * LANE, 4 * D_GROUP)
        oa_pad, s_fin = _gla(pa_pad, lb_logits[0], g_out_a[0], state_hgrn[0], n_s, LANE, LANE,
                             n_new, F32)
        oa = oa_pad.reshape(n_s, LANE, D_GROUP)[:, :n_new].reshape(rows_s, D_GROUP)
        rows_th = (n_s, n_new * N_HEADS, HEAD_DIM)
        ob = _sb_sample(pa, 4 * D_GROUP, k_new.reshape(rows_th), v_new.reshape(rows_th),
                        cache_k, cache_v, page_table, bias_rep, g_out_b[0], n_s, n_new)
        return oa, ob, s_fin

    y_s, k_s, v_s, s_s, _ = layer(x_sample.reshape(rows_s, d), mods_s, fmods_s,
                                  spec_s, spec_s1, rows_s, 1408, rows_s, mixer_s, w_ffn2)

    return (y_p.reshape(n_p, seq, d), y_s.reshape(n_s, n_new, d),
            k_p.reshape(1, n_p, seq, N_HEADS, HEAD_DIM), v_p.reshape(1, n_p, seq, N_HEADS, HEAD_DIM),
            k_s.reshape(1, n_s, n_new, N_HEADS, HEAD_DIM), v_s.reshape(1, n_s, n_new, N_HEADS, HEAD_DIM),
            s_p[None], s_s[None])
```

```python
import functools

import jax
import jax.numpy as jnp
from jax import lax
from jax.experimental import pallas as pl
from jax.experimental.pallas import tpu as pltpu

F32 = jnp.float32
BF16 = jnp.bfloat16

N_HEADS = 8
HEAD_DIM = 128
D_GROUP = N_HEADS * HEAD_DIM
N_MOD = 9
GLA_CHUNK = 32
GLA_GROUPS = 4
PAGE = 128
EPS = 1e-6
ATT_SCALE = HEAD_DIM ** -0.5
LOG2E = 1.4426950408889634
MASKED = -1e30
MOD_ROWS = 16
LANE = 128
SB_TILE = 256
SB_PAGES = 8
VMEM_LIMIT = 56 * 1024 * 1024

NT_DIMS = (((1,), (1,)), ((), ()))


def _params(sem):
    return pltpu.CompilerParams(dimension_semantics=sem, vmem_limit_bytes=VMEM_LIMIT)


def _sigmoid(x):
    return 1.0 / (1.0 + jnp.exp(-x))


def _silu(x):
    return x * _sigmoid(x)


def _adaln(x, nw, shift, scale):
    ms = jnp.mean(x * x, axis=-1, keepdims=True)
    return x * lax.rsqrt(ms + EPS) * (nw * (1.0 + scale)) + shift


def _head_norm(o, gain):
    ms = jnp.mean(o * o, axis=-1, keepdims=True)
    return o * lax.rsqrt(ms + EPS) * gain


def _split2(x):
    hi = x.astype(BF16)
    lo = (x - hi.astype(F32)).astype(BF16)
    return hi, lo


def _mod_kernel(c_ref, w_ref, b_ref, o_ref):
    a = _silu(c_ref[...]).astype(BF16)
    o_ref[...] = jnp.dot(a, w_ref[...].astype(BF16), preferred_element_type=F32) + b_ref[...]


def _modulation(c_rows, w, b, tn):
    d, n = w.shape
    return pl.pallas_call(
        _mod_kernel,
        grid=(n // tn,),
        in_specs=[
            pl.BlockSpec((MOD_ROWS, d), lambda j: (0, 0)),
            pl.BlockSpec((d, tn), lambda j: (0, j)),
            pl.BlockSpec((1, tn), lambda j: (0, j)),
        ],
        out_specs=pl.BlockSpec((MOD_ROWS, tn), lambda j: (0, j)),
        out_shape=jax.ShapeDtypeStruct((MOD_ROWS, n), F32),
        compiler_params=_params(("arbitrary",)),
        name="modulation",
    )(c_rows, w, b.reshape(1, n))


def _ffn_kernel(x_ref, nw_ref, sh_ref, sc_ref, ga_ref, wg_ref, wu_ref, wd_ref,
                nw2_ref, sh2_ref, sc2_ref, *rest, final, emit):
    n_out = (1 if final else 2) + (3 if emit else 0)
    outs, (h_scr, acc_scr) = rest[:n_out], rest[n_out:]
    y_ref = outs[0] if final else outs[1]
    f = pl.program_id(1)

    @pl.when(f == 0)
    def _():
        h = _adaln(x_ref[...], nw_ref[...], sh_ref[...], sc_ref[...])
        h_scr[...] = h.astype(BF16)
        acc_scr[...] = jnp.zeros_like(acc_scr)

    wg, wu, wd = wg_ref[...], wu_ref[...], wd_ref[...]
    if emit:
        wg, wu, wd = wg.astype(BF16), wu.astype(BF16), wd.astype(BF16)
        outs[-3][...], outs[-2][...], outs[-1][...] = wg, wu, wd
    h = h_scr[...]
    g = jnp.dot(h, wg, preferred_element_type=F32)
    u = jnp.dot(h, wu, preferred_element_type=F32)
    a = (_silu(g) * u).astype(BF16)
    acc_scr[...] += jnp.dot(a, wd, preferred_element_type=F32)

    @pl.when(f == pl.num_programs(1) - 1)
    def _():
        xn = x_ref[...] + 0.5 * ga_ref[...] * acc_scr[...]
        y = _adaln(xn, nw2_ref[...], sh2_ref[...], sc2_ref[...])
        if not final:
            outs[0][...] = xn
        y_ref[...] = y.astype(y_ref.dtype)


def _ffn(x, nw, mods, wg, wu, wd, nw2, mods2, mod_spec, tm, tf, final, emit):
    n, d = x.shape
    nf = wg.shape[1]
    assert not emit or n == tm
    row = pl.BlockSpec((tm, d), lambda i, f: (i, 0))
    vec = pl.BlockSpec((1, d), lambda i, f: (0, 0))
    w_specs = [pl.BlockSpec((d, tf), lambda i, f: (0, f)),
               pl.BlockSpec((d, tf), lambda i, f: (0, f)),
               pl.BlockSpec((tf, d), lambda i, f: (f, 0))]
    in_specs = [row, vec, mod_spec, mod_spec, mod_spec, *w_specs, vec, mod_spec, mod_spec]
    out_specs = [row] if final else [row, row]
    out_shape = ([jax.ShapeDtypeStruct((n, d), F32)] if final else
                 [jax.ShapeDtypeStruct((n, d), F32), jax.ShapeDtypeStruct((n, d), BF16)])
    if emit:
        out_specs += w_specs
        out_shape += [jax.ShapeDtypeStruct(w.shape, BF16) for w in (wg, wu, wd)]
    return pl.pallas_call(
        functools.partial(_ffn_kernel, final=final, emit=emit),
        grid=(n // tm, nf // tf),
        in_specs=in_specs,
        out_specs=out_specs,
        out_shape=out_shape,
        scratch_shapes=[pltpu.VMEM((tm, d), BF16), pltpu.VMEM((tm, d), F32)],
        compiler_params=_params(("arbitrary", "arbitrary")),
        name="ffn_final" if final else "ffn",
    )(x, nw.reshape(1, d), mods[0], mods[1], mods[2], wg, wu, wd,
      nw2.reshape(1, d), mods2[0], mods2[1])


def _in_proj_kernel(h_ref, w_ref, o_ref, wb_scr):
    @pl.when(pl.program_id(1) == 0)
    def _():
        wb_scr[...] = w_ref[...].astype(BF16)

    o_ref[...] = jnp.dot(h_ref[...], wb_scr[...], preferred_element_type=F32)


def _in_proj(h, w, col0, ncols, tm, tn):
    n, d = h.shape
    off = col0 // tn
    return pl.pallas_call(
        _in_proj_kernel,
        grid=(ncols // tn, n // tm),
        in_specs=[
            pl.BlockSpec((tm, d), lambda j, i: (i, 0)),
            pl.BlockSpec((d, tn), lambda j, i: (0, j + off)),
        ],
        out_specs=pl.BlockSpec((tm, tn), lambda j, i: (i, j)),
        out_shape=jax.ShapeDtypeStruct((n, ncols), F32),
        scratch_shapes=[pltpu.VMEM((d, tn), BF16)],
        compiler_params=_params(("arbitrary", "arbitrary")),
        name="in_proj",
    )(h, w)


def _out_proj_kernel(oa_ref, ob_ref, w_ref, x_ref, ga_ref, o_ref):
    m = jnp.dot(oa_ref[...].astype(BF16), w_ref[:D_GROUP, :], preferred_element_type=F32)
    m += jnp.dot(ob_ref[...].astype(BF16), w_ref[D_GROUP:, :], preferred_element_type=F32)
    o_ref[...] = x_ref[...] + ga_ref[...] * m


def _out_proj(oa, ob, w, x, gate, mod_spec2, tm):
    n, d = x.shape
    half = pl.BlockSpec((tm, D_GROUP), lambda i: (i, 0))
    row = pl.BlockSpec((tm, d), lambda i: (i, 0))
    return pl.pallas_call(
        _out_proj_kernel,
        grid=(n // tm,),
        in_specs=[half, half, pl.BlockSpec((2 * D_GROUP, d), lambda i: (0, 0)), row, mod_spec2],
        out_specs=row,
        out_shape=jax.ShapeDtypeStruct((n, d), F32),
        compiler_params=_params(("arbitrary",)),
        name="out_proj",
    )(oa, ob, w, x, gate)


def _gla_kernel(q_ref, f_ref, i_ref, g_ref, lbl_ref, gain_ref, s0_ref, o_ref, s_ref, st_scr,
                *, chunk, n_valid, n_groups):
    t = pl.program_id(1)
    n_chunks = LANE // chunk
    shift = chunk.bit_length() - 1

    @pl.when(t == 0)
    def _():
        for h in range(N_HEADS):
            st_scr[h] = s0_ref[h].T

    r_id = lax.broadcasted_iota(jnp.int32, (LANE, LANE), 0)
    c_id = lax.broadcasted_iota(jnp.int32, (LANE, LANE), 1)
    same = (r_id >> shift) == (c_id >> shift)
    causal = same & (c_id <= r_id)
    m_cum = jnp.concatenate([jnp.where(causal, 1.0, 0.0), jnp.where(same, 1.0, 0.0)],
                            axis=0).astype(BF16)

    heads = [slice(h * HEAD_DIM, (h + 1) * HEAD_DIM) for h in range(N_HEADS)]
    lbl = lbl_ref[...]
    mx = jnp.maximum(lbl, 0.0)
    e1 = jnp.exp(lbl - mx)
    lb = e1 / (e1 + jnp.exp(-mx))

    rows = [slice(g * LANE, (g + 1) * LANE) for g in range(n_groups)]
    kk, bb, v, vb, qd, k_inv, k_end, decay, o, v_t = ([None] * n_groups for _ in range(10))
    st = [st_scr[h] for h in range(N_HEADS)]

    def gates(g):
        f = lb + (1.0 - lb) * _sigmoid(f_ref[rows[g], :])
        logf = jnp.log(f)
        k = 1.0 - f
        if n_valid < LANE:
            valid = lax.broadcasted_iota(jnp.int32, (LANE, D_GROUP), 0) < n_valid
            logf = jnp.where(valid, logf, 0.0)
            k = jnp.where(valid, k, 0.0)
        hi = logf.astype(BF16)
        r1 = logf - hi.astype(F32)
        mid = r1.astype(BF16)
        lo = (r1 - mid.astype(F32)).astype(BF16)
        kk[g] = k
        bb[g] = jnp.dot(m_cum, jnp.concatenate([hi, mid, lo], axis=1),
                        preferred_element_type=F32)

    def decays(g):
        s = bb[g][:, :D_GROUP] + bb[g][:, D_GROUP:2 * D_GROUP] + bb[g][:, 2 * D_GROUP:]
        b = s[:LANE]
        b_last = s[LANE:]
        v[g] = i_ref[rows[g], :]
        vb[g] = v[g].astype(BF16)
        qd[g] = (q_ref[rows[g], :] * ATT_SCALE * jnp.exp(b)).astype(BF16)
        k_inv[g] = (kk[g] * jnp.exp(-b)).astype(BF16)
        k_end[g] = (kk[g] * jnp.exp(b_last - b)).astype(BF16)
        decay[g] = jnp.exp(b_last)

    def intra(g):
        a = [lax.dot_general(qd[g][:, c], k_inv[g][:, c], NT_DIMS, preferred_element_type=F32)
             for c in heads]
        a = [jnp.where(causal, x, 0.0).astype(BF16) for x in a]
        o[g] = [jnp.dot(x, vb[g][:, c], preferred_element_type=F32) for x, c in zip(a, heads)]
        v_t[g] = [v[g][:, c].T for c in heads]

    def state(g):
        o_state = [[] for _ in heads]
        for j in range(n_chunks):
            for h, c in enumerate(heads):
                o_state[h].append(lax.dot_general(qd[g][j * chunk:(j + 1) * chunk, c],
                                                  st[h].astype(BF16), NT_DIMS,
                                                  preferred_element_type=F32))
                if n_chunks > 1:
                    v_tj = jnp.where((c_id >> shift) == j, v_t[g][h], 0.0).astype(BF16)
                else:
                    v_tj = v_t[g][h].astype(BF16)
                upd = jnp.dot(v_tj, k_end[g][:, c], preferred_element_type=F32)
                st[h] = decay[g][j * chunk:j * chunk + 1, c] * st[h] + upd
        for h in range(N_HEADS):
            o[g][h] = o[g][h] + (jnp.concatenate(o_state[h], axis=0) if n_chunks > 1
                                 else o_state[h][0])

    def emit(g):
        for h, c in enumerate(heads):
            o_ref[rows[g], c] = (_head_norm(o[g][h], gain_ref[:, c])
                                 * _silu(g_ref[rows[g], c])).astype(o_ref.dtype)

    _skewed(n_groups, [gates, decays, intra, state, emit])
    for h in range(N_HEADS):
        st_scr[h] = st[h]

    @pl.when(t == pl.num_programs(1) - 1)
    def _():
        for h in range(N_HEADS):
            s_ref[h] = st_scr[h].T


def _gla(pa, lb_logits, gain, s0, n_seq, rows, chunk, n_valid, out_dtype):
    n_groups = min(GLA_GROUPS, rows // LANE)
    tile = n_groups * LANE
    n_tiles = rows // tile

    def col(group):
        return pl.BlockSpec((tile, D_GROUP), lambda b, t: (b * n_tiles + t, group))
    vec = pl.BlockSpec((1, D_GROUP), lambda b, t: (0, 0))
    state = pl.BlockSpec((None, N_HEADS, HEAD_DIM, HEAD_DIM), lambda b, t: (b, 0, 0, 0))
    return pl.pallas_call(
        functools.partial(_gla_kernel, chunk=chunk, n_valid=n_valid, n_groups=n_groups),
        grid=(n_seq, n_tiles),
        in_specs=[col(0), col(1), col(2), col(3), vec, vec, state],
        out_specs=(pl.BlockSpec((tile, D_GROUP), lambda b, t: (b * n_tiles + t, 0)), state),
        out_shape=(jax.ShapeDtypeStruct((n_seq * rows, D_GROUP), out_dtype),
                   jax.ShapeDtypeStruct((n_seq, N_HEADS, HEAD_DIM, HEAD_DIM), F32)),
        scratch_shapes=[pltpu.VMEM((N_HEADS, HEAD_DIM, HEAD_DIM), F32)],
        compiler_params=_params(("arbitrary", "arbitrary")),
        name="hgrn2",
    )(pa, pa, pa, pa, lb_logits.reshape(1, D_GROUP), gain.reshape(1, D_GROUP), s0)


def _suffix_weights():
    r_id = lax.broadcasted_iota(jnp.int32, (2 * LANE, 2 * LANE), 0) & (LANE - 1)
    c_id = lax.broadcasted_iota(jnp.int32, (2 * LANE, 2 * LANE), 1)
    return jnp.where((r_id >= c_id) | (c_id >= LANE), 1.0, 0.0).astype(BF16)


def _softplus2(z):
    return jnp.maximum(z, 0.0) + jnp.log(1.0 + jnp.exp2(-jnp.abs(z))) * LOG2E


def _sb_softplus(z_blocks):
    sp = [_softplus2(z) for z in z_blocks]
    hi, lo = _split2(sp[0] if len(sp) == 1 else jnp.concatenate(sp, axis=0))
    return jnp.concatenate([hi, lo], axis=1)


def _sb_weights(z_blocks, rt, run):
    m = z_blocks[0].shape[0]
    out = []
    for n, z in enumerate(z_blocks):
        within = rt[n * m:(n + 1) * m, :LANE]
        total = rt[n * m:(n + 1) * m, LANE:]
        out.append(jnp.exp2(z - within if run is None else z - within - run))
        run = total if run is None else run + total
    return out, run


def _skewed(n_items, stages):
    for tick in range(n_items + len(stages) - 1):
        for s in range(len(stages) - 1, -1, -1):
            i = tick - s
            if 0 <= i < n_items:
                stages[s](i)


def _sbp_kernel(q_ref, k_ref, v_ref, bias_ref, gain_ref, o_ref, q_scr, acc_scr, run_scr):
    i = pl.program_id(1)
    w2 = _suffix_weights()
    q_scr[...] = (q_ref[...] * (ATT_SCALE * LOG2E)).astype(BF16)
    acc_scr[...] = jnp.zeros_like(acc_scr)
    run_scr[...] = jnp.zeros_like(run_scr)
    r_id = lax.broadcasted_iota(jnp.int32, (SB_TILE, LANE), 0)
    c_id = lax.broadcasted_iota(jnp.int32, (SB_TILE, LANE), 1)
    n_blk = SB_TILE // LANE

    def slab(j, diagonal):
        r = pl.ds(pl.multiple_of(j * SB_TILE, SB_TILE), SB_TILE)
        order = range(n_blk - 1, -1, -1)
        if diagonal:
            mask = [jnp.where(c_id + n * LANE < r_id, 0.0, MASKED) for n in order]
        heads = [slice(h * HEAD_DIM, (h + 1) * HEAD_DIM) for h in range(N_HEADS)]
        zb, hl, rt = [None] * N_HEADS, [None] * N_HEADS, [None] * N_HEADS

        def scores(h):
            kb = k_ref[r, heads[h]].astype(BF16)
            z = lax.dot_general(q_scr[:, heads[h]], kb, NT_DIMS,
                                preferred_element_type=F32) + bias_ref[h]
            zb[h] = [z[:, n * LANE:(n + 1) * LANE] for n in order]
            if diagonal:
                zb[h] = [z_n + m_n for z_n, m_n in zip(zb[h], mask)]

        def softplus(h):
            hl[h] = _sb_softplus(zb[h])

        def suffix(h):
            rt[h] = jnp.dot(hl[h], w2, preferred_element_type=F32)

        def accumulate(h):
            a_blocks, total = _sb_weights(zb[h], rt[h], None)
            a = jnp.concatenate(a_blocks[::-1], axis=1).astype(BF16)
            p = jnp.dot(a, v_ref[r, heads[h]].astype(BF16), preferred_element_type=F32)
            run = run_scr[h]
            acc_scr[h] += p * jnp.exp2(-run)
            run_scr[h] = run + total

        _skewed(N_HEADS, [scores, softplus, suffix, accumulate])

    slab(i, True)

    def body(t, carry):
        slab(i - 1 - t, False)
        return carry
    lax.fori_loop(0, i, body, 0)

    for h in range(N_HEADS):
        c = slice(h * HEAD_DIM, (h + 1) * HEAD_DIM)
        o_ref[:, c] = _head_norm(acc_scr[h], gain_ref[:, c]).astype(o_ref.dtype)


def _sb_prompt(pq, q_col0, k, v, bias_rows, gain, n_seq, seq):
    nq = seq // SB_TILE
    qoff = q_col0 // D_GROUP
    kv = pl.BlockSpec((seq, D_GROUP), lambda b, i: (b, 0))
    return pl.pallas_call(
        _sbp_kernel,
        grid=(n_seq, nq),
        in_specs=[
            pl.BlockSpec((SB_TILE, D_GROUP), lambda b, i: (b * nq + i, qoff)),
            kv, kv,
            pl.BlockSpec((N_HEADS, 1, SB_TILE), lambda b, i: (0, 0, 0)),
            pl.BlockSpec((1, D_GROUP), lambda b, i: (0, 0)),
        ],
        out_specs=pl.BlockSpec((SB_TILE, D_GROUP), lambda b, i: (b * nq + i, 0)),
        out_shape=jax.ShapeDtypeStruct((n_seq * seq, D_GROUP), BF16),
        scratch_shapes=[pltpu.VMEM((SB_TILE, D_GROUP), BF16),
                        pltpu.VMEM((N_HEADS, SB_TILE, HEAD_DIM), F32),
                        pltpu.VMEM((N_HEADS, SB_TILE, LANE), F32)],
        compiler_params=_params(("arbitrary", "arbitrary")),
        name="stickbreak_prompt",
    )(pq, k, v, bias_rows, gain.reshape(1, D_GROUP))


def _sbs_kernel(pt_ref, q_ref, kn_ref, vn_ref, *rest, n_new):
    del pt_ref
    kc_refs = rest[:SB_PAGES]
    vc_refs = rest[SB_PAGES:2 * SB_PAGES]
    bias_ref, gain_ref, o_ref, q_scr, acc_scr, run_scr = rest[2 * SB_PAGES:]
    j = pl.program_id(1)
    nr = N_HEADS * n_new
    w2 = _suffix_weights()
    row = lax.broadcasted_iota(jnp.int32, (nr, LANE), 0)
    col = lax.broadcasted_iota(jnp.int32, (nr, LANE), 1)
    own = (col & (N_HEADS - 1)) == (row >> (n_new.bit_length() - 1))

    def visit(k_rows, v_rows, bias):
        n = len(k_rows)
        q = q_scr[...]
        zb, hl, rt, a = [None] * n, [None] * n, [None] * n, [None] * n
        state = {"run": run_scr[...], "acc": acc_scr[...]}

        def scores(p):
            z = lax.dot_general(q, k_rows[p](), NT_DIMS, preferred_element_type=F32)
            zb[p] = [z[:, m * LANE:(m + 1) * LANE] + bias
                     for m in range(z.shape[1] // LANE - 1, -1, -1)]

        def softplus(p):
            hl[p] = _sb_softplus(zb[p])

        def suffix(p):
            rt[p] = jnp.dot(hl[p], w2, preferred_element_type=F32)

        def weights(p):
            blocks, state["run"] = _sb_weights(zb[p], rt[p], state["run"])
            blocks = blocks[::-1]
            a[p] = (blocks[0] if len(blocks) == 1
                    else jnp.concatenate(blocks, axis=1)).astype(BF16)

        def accumulate(p):
            state["acc"] = state["acc"] + jnp.dot(a[p], v_rows[p](), preferred_element_type=F32)

        _skewed(n, [scores, softplus, suffix, weights, accumulate])
        acc_scr[...] = state["acc"]
        run_scr[...] = state["run"]

    @pl.when(j == 0)
    def _():
        q = q_ref[...] * (ATT_SCALE * LOG2E)
        q_scr[...] = jnp.concatenate(
            [q[:, h * HEAD_DIM:(h + 1) * HEAD_DIM] for h in range(N_HEADS)], axis=0).astype(BF16)
        run_scr[...] = jnp.zeros_like(run_scr)
        acc_scr[...] = jnp.zeros_like(acc_scr)
        pad = jnp.zeros((LANE - nr, HEAD_DIM), F32)
        kb = jnp.concatenate([kn_ref[...], pad], axis=0).astype(BF16)
        vb = jnp.concatenate([vn_ref[...], pad], axis=0).astype(BF16)
        earlier = (col >> (N_HEADS.bit_length() - 1)) < (row & (n_new - 1))
        visit([lambda: kb], [lambda: vb], jnp.where(own & earlier, bias_ref[...], MASKED))

    def page_rows(ref):
        return lambda: ref[...].reshape(PAGE * N_HEADS, HEAD_DIM).astype(BF16)
    visit([page_rows(r) for r in kc_refs], [page_rows(r) for r in vc_refs],
          jnp.where(own, bias_ref[...], MASKED))

    @pl.when(j == pl.num_programs(1) - 1)
    def _():
        acc = acc_scr[...]
        heads = [acc[h * n_new:(h + 1) * n_new] for h in range(N_HEADS)]
        heads = [o * lax.rsqrt(jnp.mean(o * o, axis=-1, keepdims=True) + EPS) for o in heads]
        o_ref[...] = jnp.concatenate(heads, axis=1) * gain_ref[...]


def _sb_sample(pq, q_col0, k_new, v_new, cache_k, cache_v, page_table, bias_rep, gain,
               n_seq, n_new):
    n_pages = page_table.shape[1]
    qoff = q_col0 // D_GROUP
    nr = N_HEADS * n_new
    assert nr <= LANE
    new = pl.BlockSpec((None, nr, HEAD_DIM), lambda b, j, pt: (b, 0, 0))

    def page(r):
        return pl.BlockSpec(
            (None, None, PAGE, N_HEADS, HEAD_DIM),
            lambda b, j, pt: (0, pt[b, n_pages - 1 - (j * SB_PAGES + r)], 0, 0, 0))
    pages = [page(r) for r in range(SB_PAGES)]
    grid_spec = pltpu.PrefetchScalarGridSpec(
        num_scalar_prefetch=1,
        grid=(n_seq, n_pages // SB_PAGES),
        in_specs=[pl.BlockSpec((n_new, D_GROUP), lambda b, j, pt: (b, qoff)), new, new,
                  *pages, *pages,
                  pl.BlockSpec((nr, LANE), lambda b, j, pt: (0, 0)),
                  pl.BlockSpec((1, D_GROUP), lambda b, j, pt: (0, 0))],
        out_specs=pl.BlockSpec((n_new, D_GROUP), lambda b, j, pt: (b, 0)),
        scratch_shapes=[pltpu.VMEM((nr, HEAD_DIM), BF16), pltpu.VMEM((nr, HEAD_DIM), F32),
                        pltpu.VMEM((nr, LANE), F32)],
    )
    return pl.pallas_call(
        functools.partial(_sbs_kernel, n_new=n_new),
        grid_spec=grid_spec,
        out_shape=jax.ShapeDtypeStruct((n_seq * n_new, D_GROUP), F32),
        compiler_params=_params(("arbitrary", "arbitrary")),
        name="stickbreak_sample",
    )(page_table, pq, k_new, v_new, *([cache_k] * SB_PAGES), *([cache_v] * SB_PAGES),
      bias_rep, gain.reshape(1, D_GROUP))


def kernel(x_prompt, x_sample, cache_k, cache_v, state_hgrn, page_table, c_prompt, c_sample,
           lb_logits, norm_ffn1, norm_mix, norm_ffn2, w_mod, b_mod,
           w_ffn1_gate, w_ffn1_up, w_ffn1_down, w_in, g_out_a, g_out_b, b_sb, w_out,
           w_ffn2_gate, w_ffn2_up, w_ffn2_down, norm_final, w_final_mod, b_final_mod):
    n_p, seq, d = x_prompt.shape
    n_s, n_new, _ = x_sample.shape
    assert w_mod.shape[0] == 1, "single-layer trunk"
    assert n_p + n_s <= MOD_ROWS
    assert n_new & (n_new - 1) == 0 and n_new <= PAGE
    assert page_table.shape[1] % SB_PAGES == 0 and seq % SB_TILE == 0

    c_rows = jnp.concatenate(
        [c_prompt, c_sample, jnp.zeros((MOD_ROWS - n_p - n_s, d), F32)], axis=0)
    mod = _modulation(c_rows, w_mod[0], b_mod[0], 1024).reshape(MOD_ROWS, N_MOD, d)
    fmod = _modulation(c_rows, w_final_mod, b_final_mod, 1024).reshape(MOD_ROWS, 2, d)

    w_out_b = w_out[0].astype(BF16)
    bias2 = b_sb[0] * LOG2E
    bias_rows = jnp.broadcast_to(bias2.reshape(N_HEADS, 1, 1), (N_HEADS, 1, SB_TILE))
    bias_rep = jnp.broadcast_to(jnp.repeat(bias2, n_new)[:, None], (N_HEADS * n_new, LANE))

    def layer(x, mods, fmods, mod_spec, mod_spec1, tm, tf, tm_proj, mixer, w_ffn1, w_ffn2, emit):
        x1, h2, *w1 = _ffn(x, norm_ffn1[0], mods[0:3], *w_ffn1, norm_mix[0], mods[3:5],
                           mod_spec, tm, tf, final=False, emit=emit)
        pa = _in_proj(h2, w_in[0], 0, 5 * D_GROUP, tm_proj, 1024)
        k_new = _in_proj(h2, w_in[0], 5 * D_GROUP, D_GROUP, tm_proj, 1024)
        v_new = _in_proj(h2, w_in[0], 6 * D_GROUP, D_GROUP, tm_proj, 1024)
        oa, ob, s_fin = mixer(pa, k_new, v_new)
        x2 = _out_proj(oa, ob, w_out_b, x1, mods[5], mod_spec1, tm)
        y, *w2 = _ffn(x2, norm_ffn2[0], mods[6:9], *w_ffn2, norm_final, fmods,
                      mod_spec, tm, tf, final=True, emit=emit)
        return y, k_new, v_new, s_fin, w1, w2

    tm_p = 512
    tiles_per_seq = seq // tm_p
    mods_p = [mod[:n_p, j].reshape(n_p, 1, d) for j in range(N_MOD)]
    fmods_p = [fmod[:n_p, j].reshape(n_p, 1, d) for j in range(2)]
    spec_p = pl.BlockSpec((None, 1, d), lambda i, f: (i // tiles_per_seq, 0, 0))
    spec_p1 = pl.BlockSpec((None, 1, d), lambda i: (i // tiles_per_seq, 0, 0))

    def mixer_p(pa, k_new, v_new):
        s0 = jnp.zeros((n_p, N_HEADS, HEAD_DIM, HEAD_DIM), F32)
        oa, s_fin = _gla(pa, lb_logits[0], g_out_a[0], s0, n_p, seq, GLA_CHUNK, LANE, BF16)
        ob = _sb_prompt(pa, 4 * D_GROUP, k_new, v_new, bias_rows, g_out_b[0], n_p, seq)
        return oa, ob, s_fin

    rows_s = n_s * n_new
    mods_s = [jnp.repeat(mod[n_p:n_p + n_s, j], n_new, axis=0).reshape(1, rows_s, d)
              for j in range(N_MOD)]
    fmods_s = [jnp.repeat(fmod[n_p:n_p + n_s, j], n_new, axis=0).reshape(1, rows_s, d)
               for j in range(2)]
    spec_s = pl.BlockSpec((None, rows_s, d), lambda i, f: (0, 0, 0))
    spec_s1 = pl.BlockSpec((None, rows_s, d), lambda i: (0, 0, 0))

    def mixer_s(pa, k_new, v_new):
        pa_pad = jnp.pad(pa[:, :4 * D_GROUP].reshape(n_s, n_new, 4 * D_GROUP),
                         ((0, 0), (0, LANE - n_new), (0, 0))).reshape(n_s * LANE, 4 * D_GROUP)
        oa_pad, s_fin = _gla(pa_pad, lb_logits[0], g_out_a[0], state_hgrn[0], n_s, LANE, LANE,
                             n_new, F32)
        oa = oa_pad.reshape(n_s, LANE, D_GROUP)[:, :n_new].reshape(rows_s, D_GROUP)
        rows_th = (n_s, n_new * N_HEADS, HEAD_DIM)
        ob = _sb_sample(pa, 4 * D_GROUP, k_new.reshape(rows_th), v_new.reshape(rows_th),
                        cache_k, cache_v, page_table, bias_rep, g_out_b[0], n_s, n_new)
        return oa, ob, s_fin

    y_s, k_s, v_s, s_s, w1, w2 = layer(
        x_sample.reshape(rows_s, d), mods_s, fmods_s, spec_s, spec_s1, rows_s, 512, rows_s,
        mixer_s, (w_ffn1_gate[0], w_ffn1_up[0], w_ffn1_down[0]),
        (w_ffn2_gate[0], w_ffn2_up[0], w_ffn2_down[0]), True)
    y_p, k_p, v_p, s_p, _, _ = layer(
        x_prompt.reshape(n_p * seq, d), mods_p, fmods_p, spec_p, spec_p1, tm_p, 512, 1024,
        mixer_p, w1, w2, False)

    return (y_p.reshape(n_p, seq, d), y_s.reshape(n_s, n_new, d),
            k_p.reshape(1, n_p, seq, N_HEADS, HEAD_DIM), v_p.reshape(1, n_p, seq, N_HEADS, HEAD_DIM),
            k_s.reshape(1, n_s, n_new, N_HEADS, HEAD_DIM), v_s.reshape(1, n_s, n_new, N_HEADS, HEAD_DIM),
            s_p[None], s_s[None])
```

```python
import functools

import jax
import jax.numpy as jnp
from jax import lax
from jax.experimental import pallas as pl
from jax.experimental.pallas import tpu as pltpu

F32 = jnp.float32
BF16 = jnp.bfloat16

N_HEADS = 8
HEAD_DIM = 128
D_GROUP = N_HEADS * HEAD_DIM
N_MOD = 9
GLA_CHUNK = 32
GLA_GROUPS = 8
PAGE = 128
EPS = 1e-6
ATT_SCALE = HEAD_DIM ** -0.5
LOG2E = 1.4426950408889634
MASKED = -1e30
MOD_ROWS = 16
LANE = 128
SB_TILE = 256
SB_PAGES = 8
VMEM_LIMIT = 56 * 1024 * 1024

NT_DIMS = (((1,), (1,)), ((), ()))


def _params(sem):
    return pltpu.CompilerParams(dimension_semantics=sem, vmem_limit_bytes=VMEM_LIMIT)


def _sigmoid(x):
    return 1.0 / (1.0 + jnp.exp(-x))


def _silu(x):
    return x * _sigmoid(x)


def _adaln(x, nw, shift, scale):
    ms = jnp.mean(x * x, axis=-1, keepdims=True)
    return x * lax.rsqrt(ms + EPS) * (nw * (1.0 + scale)) + shift


def _head_norm(o, gain):
    ms = jnp.mean(o * o, axis=-1, keepdims=True)
    return o * lax.rsqrt(ms + EPS) * gain


def _split2(x):
    hi = x.astype(BF16)
    lo = (x - hi.astype(F32)).astype(BF16)
    return hi, lo


def _mod_kernel(c_ref, w_ref, b_ref, o_ref):
    a = _silu(c_ref[...]).astype(BF16)
    o_ref[...] = jnp.dot(a, w_ref[...].astype(BF16), preferred_element_type=F32) + b_ref[...]


def _modulation(c_rows, w, b, tn):
    d, n = w.shape
    return pl.pallas_call(
        _mod_kernel,
        grid=(n // tn,),
        in_specs=[
            pl.BlockSpec((MOD_ROWS, d), lambda j: (0, 0)),
            pl.BlockSpec((d, tn), lambda j: (0, j)),
            pl.BlockSpec((1, tn), lambda j: (0, j)),
        ],
        out_specs=pl.BlockSpec((MOD_ROWS, tn), lambda j: (0, j)),
        out_shape=jax.ShapeDtypeStruct((MOD_ROWS, n), F32),
        compiler_params=_params(("arbitrary",)),
        name="modulation",
    )(c_rows, w, b.reshape(1, n))


def _ffn_kernel(x_ref, nw_ref, sh_ref, sc_ref, ga_ref, wg_ref, wu_ref, wd_ref,
                nw2_ref, sh2_ref, sc2_ref, *rest, final, emit):
    n_out = (1 if final else 2) + (3 if emit else 0)
    outs, (h_scr, acc_scr) = rest[:n_out], rest[n_out:]
    y_ref = outs[0] if final else outs[1]
    f = pl.program_id(1)

    @pl.when(f == 0)
    def _():
        h = _adaln(x_ref[...], nw_ref[...], sh_ref[...], sc_ref[...])
        h_scr[...] = h.astype(BF16)
        acc_scr[...] = jnp.zeros_like(acc_scr)

    wg, wu, wd = wg_ref[...], wu_ref[...], wd_ref[...]
    if emit:
        wg, wu, wd = wg.astype(BF16), wu.astype(BF16), wd.astype(BF16)
        outs[-3][...], outs[-2][...], outs[-1][...] = wg, wu, wd
    h = h_scr[...]
    g = jnp.dot(h, wg, preferred_element_type=F32)
    u = jnp.dot(h, wu, preferred_element_type=F32)
    a = (_silu(g) * u).astype(BF16)
    acc_scr[...] += jnp.dot(a, wd, preferred_element_type=F32)

    @pl.when(f == pl.num_programs(1) - 1)
    def _():
        xn = x_ref[...] + 0.5 * ga_ref[...] * acc_scr[...]
        y = _adaln(xn, nw2_ref[...], sh2_ref[...], sc2_ref[...])
        if not final:
            outs[0][...] = xn
        y_ref[...] = y.astype(y_ref.dtype)


def _ffn(x, nw, mods, wg, wu, wd, nw2, mods2, mod_spec, tm, tf, final, emit):
    n, d = x.shape
    nf = wg.shape[1]
    assert not emit or n == tm
    row = pl.BlockSpec((tm, d), lambda i, f: (i, 0))
    vec = pl.BlockSpec((1, d), lambda i, f: (0, 0))
    w_specs = [pl.BlockSpec((d, tf), lambda i, f: (0, f)),
               pl.BlockSpec((d, tf), lambda i, f: (0, f)),
               pl.BlockSpec((tf, d), lambda i, f: (f, 0))]
    in_specs = [row, vec, mod_spec, mod_spec, mod_spec, *w_specs, vec, mod_spec, mod_spec]
    out_specs = [row] if final else [row, row]
    out_shape = ([jax.ShapeDtypeStruct((n, d), F32)] if final else
                 [jax.ShapeDtypeStruct((n, d), F32), jax.ShapeDtypeStruct((n, d), BF16)])
    if emit:
        out_specs += w_specs
        out_shape += [jax.ShapeDtypeStruct(w.shape, BF16) for w in (wg, wu, wd)]
    return pl.pallas_call(
        functools.partial(_ffn_kernel, final=final, emit=emit),
        grid=(n // tm, nf // tf),
        in_specs=in_specs,
        out_specs=out_specs,
        out_shape=out_shape,
        scratch_shapes=[pltpu.VMEM((tm, d), BF16), pltpu.VMEM((tm, d), F32)],
        compiler_params=_params(("arbitrary", "arbitrary")),
        name="ffn_final" if final else "ffn",
    )(x, nw.reshape(1, d), mods[0], mods[1], mods[2], wg, wu, wd,
      nw2.reshape(1, d), mods2[0], mods2[1])


def _in_proj_kernel(h_ref, w_ref, o_ref, wb_scr):
    @pl.when(pl.program_id(1) == 0)
    def _():
        wb_scr[...] = w_ref[...].astype(BF16)

    o_ref[...] = jnp.dot(h_ref[...], wb_scr[...], preferred_element_type=F32)


def _in_proj(h, w, col0, ncols, tm, tn):
    n, d = h.shape
    off = col0 // tn
    return pl.pallas_call(
        _in_proj_kernel,
        grid=(ncols // tn, n // tm),
        in_specs=[
            pl.BlockSpec((tm, d), lambda j, i: (i, 0)),
            pl.BlockSpec((d, tn), lambda j, i: (0, j + off)),
        ],
        out_specs=pl.BlockSpec((tm, tn), lambda j, i: (i, j)),
        out_shape=jax.ShapeDtypeStruct((n, ncols), F32),
        scratch_shapes=[pltpu.VMEM((d, tn), BF16)],
        compiler_params=_params(("arbitrary", "arbitrary")),
        name="in_proj",
    )(h, w)


def _out_proj_kernel(oa_ref, ob_ref, w_ref, x_ref, ga_ref, o_ref):
    m = jnp.dot(oa_ref[...].astype(BF16), w_ref[:D_GROUP, :], preferred_element_type=F32)
    m += jnp.dot(ob_ref[...].astype(BF16), w_ref[D_GROUP:, :], preferred_element_type=F32)
    o_ref[...] = x_ref[...] + ga_ref[...] * m


def _out_proj(oa, ob, w, x, gate, mod_spec2, tm):
    n, d = x.shape
    half = pl.BlockSpec((tm, D_GROUP), lambda i: (i, 0))
    row = pl.BlockSpec((tm, d), lambda i: (i, 0))
    return pl.pallas_call(
        _out_proj_kernel,
        grid=(n // tm,),
        in_specs=[half, half, pl.BlockSpec((2 * D_GROUP, d), lambda i: (0, 0)), row, mod_spec2],
        out_specs=row,
        out_shape=jax.ShapeDtypeStruct((n, d), F32),
        compiler_params=_params(("arbitrary",)),
        name="out_proj",
    )(oa, ob, w, x, gate)


def _gla_kernel(q_ref, f_ref, i_ref, g_ref, lbl_ref, gain_ref, s0_ref, o_ref, s_ref, st_scr,
                *, chunk, n_valid, n_groups):
    t = pl.program_id(1)
    n_chunks = LANE // chunk
    shift = chunk.bit_length() - 1

    @pl.when(t == 0)
    def _():
        for h in range(N_HEADS):
            st_scr[h] = s0_ref[h].T

    r_id = lax.broadcasted_iota(jnp.int32, (LANE, LANE), 0)
    c_id = lax.broadcasted_iota(jnp.int32, (LANE, LANE), 1)
    same = (r_id >> shift) == (c_id >> shift)
    causal = same & (c_id <= r_id)
    m_cum = jnp.concatenate([jnp.where(causal, 1.0, 0.0), jnp.where(same, 1.0, 0.0)],
                            axis=0).astype(BF16)

    heads = [slice(h * HEAD_DIM, (h + 1) * HEAD_DIM) for h in range(N_HEADS)]
    lbl = lbl_ref[...]
    mx = jnp.maximum(lbl, 0.0)
    e1 = jnp.exp(lbl - mx)
    lb = e1 / (e1 + jnp.exp(-mx))

    rows = [slice(g * LANE, (g + 1) * LANE) for g in range(n_groups)]
    kk, bb, v, vb, qd, k_inv, k_end, decay, o, v_t = ([None] * n_groups for _ in range(10))
    st = [st_scr[h] for h in range(N_HEADS)]

    def gates(g):
        f = lb + (1.0 - lb) * _sigmoid(f_ref[rows[g], :])
        logf = jnp.log(f)
        k = 1.0 - f
        if n_valid < LANE:
            valid = lax.broadcasted_iota(jnp.int32, (LANE, D_GROUP), 0) < n_valid
            logf = jnp.where(valid, logf, 0.0)
            k = jnp.where(valid, k, 0.0)
        hi = logf.astype(BF16)
        r1 = logf - hi.astype(F32)
        mid = r1.astype(BF16)
        lo = (r1 - mid.astype(F32)).astype(BF16)
        kk[g] = k
        bb[g] = jnp.dot(m_cum, jnp.concatenate([hi, mid, lo], axis=1),
                        preferred_element_type=F32)

    def decays(g):
        s = bb[g][:, :D_GROUP] + bb[g][:, D_GROUP:2 * D_GROUP] + bb[g][:, 2 * D_GROUP:]
        b = s[:LANE]
        b_last = s[LANE:]
        v[g] = i_ref[rows[g], :]
        vb[g] = v[g].astype(BF16)
        qd[g] = (q_ref[rows[g], :] * ATT_SCALE * jnp.exp(b)).astype(BF16)
        k_inv[g] = (kk[g] * jnp.exp(-b)).astype(BF16)
        k_end[g] = (kk[g] * jnp.exp(b_last - b)).astype(BF16)
        decay[g] = jnp.exp(b_last)

    def intra(g):
        a = [lax.dot_general(qd[g][:, c], k_inv[g][:, c], NT_DIMS, preferred_element_type=F32)
             for c in heads]
        a = [jnp.where(causal, x, 0.0).astype(BF16) for x in a]
        o[g] = [jnp.dot(x, vb[g][:, c], preferred_element_type=F32) for x, c in zip(a, heads)]
        v_t[g] = [[v[g][j * chunk:(j + 1) * chunk, c].T.astype(BF16) for j in range(n_chunks)]
                  for c in heads]

    def state(g):
        o_state = [[] for _ in heads]
        for j in range(n_chunks):
            for h, c in enumerate(heads):
                o_state[h].append(lax.dot_general(qd[g][j * chunk:(j + 1) * chunk, c],
                                                  st[h].astype(BF16), NT_DIMS,
                                                  preferred_element_type=F32))
                upd = jnp.dot(v_t[g][h][j], k_end[g][j * chunk:(j + 1) * chunk, c],
                              preferred_element_type=F32)
                st[h] = decay[g][j * chunk:j * chunk + 1, c] * st[h] + upd
        for h in range(N_HEADS):
            o[g][h] = o[g][h] + (jnp.concatenate(o_state[h], axis=0) if n_chunks > 1
                                 else o_state[h][0])

    def emit(g):
        for h, c in enumerate(heads):
            o_ref[rows[g], c] = (_head_norm(o[g][h], gain_ref[:, c])
                                 * _silu(g_ref[rows[g], c])).astype(o_ref.dtype)

    _skewed(n_groups, [gates, decays, intra, state, emit])
    for h in range(N_HEADS):
        st_scr[h] = st[h]

    @pl.when(t == pl.num_programs(1) - 1)
    def _():
        for h in range(N_HEADS):
            s_ref[h] = st_scr[h].T


def _gla(pa, lb_logits, gain, s0, n_seq, rows, chunk, n_valid, out_dtype):
    n_groups = min(GLA_GROUPS, rows // LANE)
    tile = n_groups * LANE
    n_tiles = rows // tile

    def col(group):
        return pl.BlockSpec((tile, D_GROUP), lambda b, t: (b * n_tiles + t, group))
    vec = pl.BlockSpec((1, D_GROUP), lambda b, t: (0, 0))
    state = pl.BlockSpec((None, N_HEADS, HEAD_DIM, HEAD_DIM), lambda b, t: (b, 0, 0, 0))
    return pl.pallas_call(
        functools.partial(_gla_kernel, chunk=chunk, n_valid=n_valid, n_groups=n_groups),
        grid=(n_seq, n_tiles),
        in_specs=[col(0), col(1), col(2), col(3), vec, vec, state],
        out_specs=(pl.BlockSpec((tile, D_GROUP), lambda b, t: (b * n_tiles + t, 0)), state),
        out_shape=(jax.ShapeDtypeStruct((n_seq * rows, D_GROUP), out_dtype),
                   jax.ShapeDtypeStruct((n_seq, N_HEADS, HEAD_DIM, HEAD_DIM), F32)),
        scratch_shapes=[pltpu.VMEM((N_HEADS, HEAD_DIM, HEAD_DIM), F32)],
        compiler_params=_params(("arbitrary", "arbitrary")),
        name="hgrn2",
    )(pa, pa, pa, pa, lb_logits.reshape(1, D_GROUP), gain.reshape(1, D_GROUP), s0)


def _suffix_weights():
    r_id = lax.broadcasted_iota(jnp.int32, (2 * LANE, 2 * LANE), 0) & (LANE - 1)
    c_id = lax.broadcasted_iota(jnp.int32, (2 * LANE, 2 * LANE), 1)
    return jnp.where((r_id >= c_id) | (c_id >= LANE), 1.0, 0.0).astype(BF16)


def _softplus2(z):
    return jnp.maximum(z, 0.0) + jnp.log(1.0 + jnp.exp2(-jnp.abs(z))) * LOG2E


def _sb_softplus(z_blocks):
    sp = [_softplus2(z) for z in z_blocks]
    hi, lo = _split2(sp[0] if len(sp) == 1 else jnp.concatenate(sp, axis=0))
    return jnp.concatenate([hi, lo], axis=1)


def _sb_weights(z_blocks, rt, run):
    m = z_blocks[0].shape[0]
    out = []
    for n, z in enumerate(z_blocks):
        within = rt[n * m:(n + 1) * m, :LANE]
        total = rt[n * m:(n + 1) * m, LANE:]
        out.append(jnp.exp2(z - within if run is None else z - within - run))
        run = total if run is None else run + total
    return out, run


def _skewed(n_items, stages):
    for tick in range(n_items + len(stages) - 1):
        for s in range(len(stages) - 1, -1, -1):
            i = tick - s
            if 0 <= i < n_items:
                stages[s](i)


def _sbp_kernel(q_ref, k_ref, v_ref, bias_ref, gain_ref, o_ref, q_scr, acc_scr, run_scr):
    i = pl.program_id(1)
    w2 = _suffix_weights()
    q_scr[...] = (q_ref[...] * (ATT_SCALE * LOG2E)).astype(BF16)
    acc_scr[...] = jnp.zeros_like(acc_scr)
    run_scr[...] = jnp.zeros_like(run_scr)
    r_id = lax.broadcasted_iota(jnp.int32, (SB_TILE, LANE), 0)
    c_id = lax.broadcasted_iota(jnp.int32, (SB_TILE, LANE), 1)
    n_blk = SB_TILE // LANE

    def slab(j, diagonal):
        r = pl.ds(pl.multiple_of(j * SB_TILE, SB_TILE), SB_TILE)
        order = range(n_blk - 1, -1, -1)
        if diagonal:
            mask = [jnp.where(c_id + n * LANE < r_id, 0.0, MASKED) for n in order]
        heads = [slice(h * HEAD_DIM, (h + 1) * HEAD_DIM) for h in range(N_HEADS)]
        zb, hl, rt = [None] * N_HEADS, [None] * N_HEADS, [None] * N_HEADS

        def scores(h):
            kb = k_ref[r, heads[h]].astype(BF16)
            z = lax.dot_general(q_scr[:, heads[h]], kb, NT_DIMS,
                                preferred_element_type=F32) + bias_ref[h]
            zb[h] = [z[:, n * LANE:(n + 1) * LANE] for n in order]
            if diagonal:
                zb[h] = [z_n + m_n for z_n, m_n in zip(zb[h], mask)]

        def softplus(h):
            hl[h] = _sb_softplus(zb[h])

        def suffix(h):
            rt[h] = jnp.dot(hl[h], w2, preferred_element_type=F32)

        def accumulate(h):
            a_blocks, total = _sb_weights(zb[h], rt[h], None)
            a = jnp.concatenate(a_blocks[::-1], axis=1).astype(BF16)
            p = jnp.dot(a, v_ref[r, heads[h]].astype(BF16), preferred_element_type=F32)
            run = run_scr[h]
            acc_scr[h] += p * jnp.exp2(-run)
            run_scr[h] = run + total

        _skewed(N_HEADS, [scores, softplus, suffix, accumulate])

    slab(i, True)

    def body(t, carry):
        slab(i - 1 - t, False)
        return carry
    lax.fori_loop(0, i, body, 0)

    for h in range(N_HEADS):
        c = slice(h * HEAD_DIM, (h + 1) * HEAD_DIM)
        o_ref[:, c] = _head_norm(acc_scr[h], gain_ref[:, c]).astype(o_ref.dtype)


def _sb_prompt(pq, q_col0, k, v, bias_rows, gain, n_seq, seq):
    nq = seq // SB_TILE
    qoff = q_col0 // D_GROUP
    kv = pl.BlockSpec((seq, D_GROUP), lambda b, i: (b, 0))
    return pl.pallas_call(
        _sbp_kernel,
        grid=(n_seq, nq),
        in_specs=[
            pl.BlockSpec((SB_TILE, D_GROUP), lambda b, i: (b * nq + i, qoff)),
            kv, kv,
            pl.BlockSpec((N_HEADS, 1, SB_TILE), lambda b, i: (0, 0, 0)),
            pl.BlockSpec((1, D_GROUP), lambda b, i: (0, 0)),
        ],
        out_specs=pl.BlockSpec((SB_TILE, D_GROUP), lambda b, i: (b * nq + i, 0)),
        out_shape=jax.ShapeDtypeStruct((n_seq * seq, D_GROUP), BF16),
        scratch_shapes=[pltpu.VMEM((SB_TILE, D_GROUP), BF16),
                        pltpu.VMEM((N_HEADS, SB_TILE, HEAD_DIM), F32),
                        pltpu.VMEM((N_HEADS, SB_TILE, LANE), F32)],
        compiler_params=_params(("arbitrary", "arbitrary")),
        name="stickbreak_prompt",
    )(pq, k, v, bias_rows, gain.reshape(1, D_GROUP))


def _sbs_kernel(pt_ref, q_ref, kn_ref, vn_ref, *rest, n_new):
    del pt_ref
    kc_refs = rest[:SB_PAGES]
    vc_refs = rest[SB_PAGES:2 * SB_PAGES]
    bias_ref, gain_ref, o_ref, q_scr, acc_scr, run_scr = rest[2 * SB_PAGES:]
    j = pl.program_id(1)
    nr = N_HEADS * n_new
    w2 = _suffix_weights()
    row = lax.broadcasted_iota(jnp.int32, (nr, LANE), 0)
    col = lax.broadcasted_iota(jnp.int32, (nr, LANE), 1)
    own = (col & (N_HEADS - 1)) == (row >> (n_new.bit_length() - 1))

    def visit(k_rows, v_rows, bias):
        n = len(k_rows)
        q = q_scr[...]
        zb, hl, rt, a = [None] * n, [None] * n, [None] * n, [None] * n
        state = {"run": run_scr[...], "acc": acc_scr[...]}

        def scores(p):
            z = lax.dot_general(q, k_rows[p](), NT_DIMS, preferred_element_type=F32)
            zb[p] = [z[:, m * LANE:(m + 1) * LANE] + bias
                     for m in range(z.shape[1] // LANE - 1, -1, -1)]

        def softplus(p):
            hl[p] = _sb_softplus(zb[p])

        def suffix(p):
            rt[p] = jnp.dot(hl[p], w2, preferred_element_type=F32)

        def weights(p):
            blocks, state["run"] = _sb_weights(zb[p], rt[p], state["run"])
            blocks = blocks[::-1]
            a[p] = (blocks[0] if len(blocks) == 1
                    else jnp.concatenate(blocks, axis=1)).astype(BF16)

        def accumulate(p):
            state["acc"] = state["acc"] + jnp.dot(a[p], v_rows[p](), preferred_element_type=F32)

        _skewed(n, [scores, softplus, suffix, weights, accumulate])
        acc_scr[...] = state["acc"]
        run_scr[...] = state["run"]

    @pl.when(j == 0)
    def _():
        q = q_ref[...] * (ATT_SCALE * LOG2E)
        q_scr[...] = jnp.concatenate(
            [q[:, h * HEAD_DIM:(h + 1) * HEAD_DIM] for h in range(N_HEADS)], axis=0).astype(BF16)
        run_scr[...] = jnp.zeros_like(run_scr)
        acc_scr[...] = jnp.zeros_like(acc_scr)
        pad = jnp.zeros((LANE - nr, HEAD_DIM), F32)
        kb = jnp.concatenate([kn_ref[...], pad], axis=0).astype(BF16)
        vb = jnp.concatenate([vn_ref[...], pad], axis=0).astype(BF16)
        earlier = (col >> (N_HEADS.bit_length() - 1)) < (row & (n_new - 1))
        visit([lambda: kb], [lambda: vb], jnp.where(own & earlier, bias_ref[...], MASKED))

    def page_rows(ref):
        return lambda: ref[...].reshape(PAGE * N_HEADS, HEAD_DIM).astype(BF16)
    visit([page_rows(r) for r in kc_refs], [page_rows(r) for r in vc_refs],
          jnp.where(own, bias_ref[...], MASKED))

    @pl.when(j == pl.num_programs(1) - 1)
    def _():
        acc = acc_scr[...]
        heads = [acc[h * n_new:(h + 1) * n_new] for h in range(N_HEADS)]
        heads = [o * lax.rsqrt(jnp.mean(o * o, axis=-1, keepdims=True) + EPS) for o in heads]
        o_ref[...] = jnp.concatenate(heads, axis=1) * gain_ref[...]


def _sb_sample(pq, q_col0, k_new, v_new, cache_k, cache_v, page_table, bias_rep, gain,
               n_seq, n_new):
    n_pages = page_table.shape[1]
    qoff = q_col0 // D_GROUP
    nr = N_HEADS * n_new
    assert nr <= LANE
    new = pl.BlockSpec((None, nr, HEAD_DIM), lambda b, j, pt: (b, 0, 0))

    def page(r):
        return pl.BlockSpec(
            (None, None, PAGE, N_HEADS, HEAD_DIM),
            lambda b, j, pt: (0, pt[b, n_pages - 1 - (j * SB_PAGES + r)], 0, 0, 0))
    pages = [page(r) for r in range(SB_PAGES)]
    grid_spec = pltpu.PrefetchScalarGridSpec(
        num_scalar_prefetch=1,
        grid=(n_seq, n_pages // SB_PAGES),
        in_specs=[pl.BlockSpec((n_new, D_GROUP), lambda b, j, pt: (b, qoff)), new, new,
                  *pages, *pages,
                  pl.BlockSpec((nr, LANE), lambda b, j, pt: (0, 0)),
                  pl.BlockSpec((1, D_GROUP), lambda b, j, pt: (0, 0))],
        out_specs=pl.BlockSpec((n_new, D_GROUP), lambda b, j, pt: (b, 0)),
        scratch_shapes=[pltpu.VMEM((nr, HEAD_DIM), BF16), pltpu.VMEM((nr, HEAD_DIM), F32),
                        pltpu.VMEM((nr, LANE), F32)],
    )
    return pl.pallas_call(
        functools.partial(_sbs_kernel, n_new=n_new),
        grid_spec=grid_spec,
        out_shape=jax.ShapeDtypeStruct((n_seq * n_new, D_GROUP), F32),
        compiler_params=_params(("arbitrary", "arbitrary")),
        name="stickbreak_sample",
    )(page_table, pq, k_new, v_new, *([cache_k] * SB_PAGES), *([cache_v] * SB_PAGES),
      bias_rep, gain.reshape(1, D_GROUP))


def kernel(x_prompt, x_sample, cache_k, cache_v, state_hgrn, page_table, c_prompt, c_sample,
           lb_logits, norm_ffn1, norm_mix, norm_ffn2, w_mod, b_mod,
           w_ffn1_gate, w_ffn1_up, w_ffn1_down, w_in, g_out_a, g_out_b, b_sb, w_out,
           w_ffn2_gate, w_ffn2_up, w_ffn2_down, norm_final, w_final_mod, b_final_mod):
    n_p, seq, d = x_prompt.shape
    n_s, n_new, _ = x_sample.shape
    assert w_mod.shape[0] == 1, "single-layer trunk"
    assert n_p + n_s <= MOD_ROWS
    assert n_new & (n_new - 1) == 0 and n_new <= PAGE
    assert page_table.shape[1] % SB_PAGES == 0 and seq % SB_TILE == 0

    c_rows = jnp.concatenate(
        [c_prompt, c_sample, jnp.zeros((MOD_ROWS - n_p - n_s, d), F32)], axis=0)
    mod = _modulation(c_rows, w_mod[0], b_mod[0], 1024).reshape(MOD_ROWS, N_MOD, d)
    fmod = _modulation(c_rows, w_final_mod, b_final_mod, 1024).reshape(MOD_ROWS, 2, d)

    w_out_b = w_out[0].astype(BF16)
    bias2 = b_sb[0] * LOG2E
    bias_rows = jnp.broadcast_to(bias2.reshape(N_HEADS, 1, 1), (N_HEADS, 1, SB_TILE))
    bias_rep = jnp.broadcast_to(jnp.repeat(bias2, n_new)[:, None], (N_HEADS * n_new, LANE))

    def layer(x, mods, fmods, mod_spec, mod_spec1, tm, tf, tm_proj, mixer, w_ffn1, w_ffn2, emit):
        x1, h2, *w1 = _ffn(x, norm_ffn1[0], mods[0:3], *w_ffn1, norm_mix[0], mods[3:5],
                           mod_spec, tm, tf, final=False, emit=emit)
        pa = _in_proj(h2, w_in[0], 0, 5 * D_GROUP, tm_proj, 1024)
        k_new = _in_proj(h2, w_in[0], 5 * D_GROUP, D_GROUP, tm_proj, 1024)
        v_new = _in_proj(h2, w_in[0], 6 * D_GROUP, D_GROUP, tm_proj, 1024)
        oa, ob, s_fin = mixer(pa, k_new, v_new)
        x2 = _out_proj(oa, ob, w_out_b, x1, mods[5], mod_spec1, tm)
        y, *w2 = _ffn(x2, norm_ffn2[0], mods[6:9], *w_ffn2, norm_final, fmods,
                      mod_spec, tm, tf, final=True, emit=emit)
        return y, k_new, v_new, s_fin, w1, w2

    tm_p = 512
    tiles_per_seq = seq // tm_p
    mods_p = [mod[:n_p, j].reshape(n_p, 1, d) for j in range(N_MOD)]
    fmods_p = [fmod[:n_p, j].reshape(n_p, 1, d) for j in range(2)]
    spec_p = pl.BlockSpec((None, 1, d), lambda i, f: (i // tiles_per_seq, 0, 0))
    spec_p1 = pl.BlockSpec((None, 1, d), lambda i: (i // tiles_per_seq, 0, 0))

    def mixer_p(pa, k_new, v_new):
        s0 = jnp.zeros((n_p, N_HEADS, HEAD_DIM, HEAD_DIM), F32)
        oa, s_fin = _gla(pa, lb_logits[0], g_out_a[0], s0, n_p, seq, GLA_CHUNK, LANE, BF16)
        ob = _sb_prompt(pa, 4 * D_GROUP, k_new, v_new, bias_rows, g_out_b[0], n_p, seq)
        return oa, ob, s_fin

    rows_s = n_s * n_new
    mods_s = [jnp.repeat(mod[n_p:n_p + n_s, j], n_new, axis=0).reshape(1, rows_s, d)
              for j in range(N_MOD)]
    fmods_s = [jnp.repeat(fmod[n_p:n_p + n_s, j], n_new, axis=0).reshape(1, rows_s, d)
               for j in range(2)]
    spec_s = pl.BlockSpec((None, rows_s, d), lambda i, f: (0, 0, 0))
    spec_s1 = pl.BlockSpec((None, rows_s, d), lambda i: (0, 0, 0))

    def mixer_s(pa, k_new, v_new):
        pa_pad = jnp.pad(pa[:, :4 * D_GROUP].reshape(n_s, n_new, 4 * D_GROUP),
                         ((0, 0), (0, LANE - n_new), (0, 0))).reshape(n_s * LANE, 4 * D_GROUP)
        oa_pad, s_fin = _gla(pa_pad, lb_logits[0], g_out_a[0], state_hgrn[0], n_s, LANE, LANE,
                             n_new, F32)
        oa = oa_pad.reshape(n_s, LANE, D_GROUP)[:, :n_new].reshape(rows_s, D_GROUP)
        rows_th = (n_s, n_new * N_HEADS, HEAD_DIM)
        ob = _sb_sample(pa, 4 * D_GROUP, k_new.reshape(rows_th), v_new.reshape(rows_th),
                        cache_k, cache_v, page_table, bias_rep, g_out_b[0], n_s, n_new)
        return oa, ob, s_fin

    y_s, k_s, v_s, s_s, w1, w2 = layer(
        x_sample.reshape(rows_s, d), mods_s, fmods_s, spec_s, spec_s1, rows_s, 512, rows_s,
        mixer_s, (w_ffn1_gate[0], w_ffn1_up[0], w_ffn1_down[0]),
        (w_ffn2_gate[0], w_ffn2_up[0], w_ffn2_down[0]), True)
    y_p, k_p, v_p, s_p, _, _ = layer(
        x_prompt.reshape(n_p * seq, d), mods_p, fmods_p, spec_p, spec_p1, tm_p, 512, 1024,
        mixer_p, w1, w2, False)

    return (y_p.reshape(n_p, seq, d), y_s.reshape(n_s, n_new, d),
            k_p.reshape(1, n_p, seq, N_HEADS, HEAD_DIM), v_p.reshape(1, n_p, seq, N_HEADS, HEAD_DIM),
            k_s.reshape(1, n_s, n_new, N_HEADS, HEAD_DIM), v_s.reshape(1, n_s, n_new, N_HEADS, HEAD_DIM),
            s_p[None], s_s[None])
```

```python
import functools

import jax
import jax.numpy as jnp
from jax import lax
from jax.experimental import pallas as pl
from jax.experimental.pallas import tpu as pltpu

F32 = jnp.float32
BF16 = jnp.bfloat16

N_HEADS = 8
HEAD_DIM = 128
D_GROUP = N_HEADS * HEAD_DIM
N_MOD = 9
GLA_CHUNK = 32
GLA_GROUPS = 8
PAGE = 128
EPS = 1e-6
ATT_SCALE = HEAD_DIM ** -0.5
LOG2E = 1.4426950408889634
MASKED = -1e30
MOD_ROWS = 16
LANE = 128
SB_TILE = 256
SB_PAGES = 16
FFN_ROWS = 512
FFN_COLS = 512
PROJ_ROWS = 1024
PROJ_COLS = 1024
VMEM_LIMIT = 56 * 1024 * 1024

NT_DIMS = (((1,), (1,)), ((), ()))


def _params(sem):
    return pltpu.CompilerParams(dimension_semantics=sem, vmem_limit_bytes=VMEM_LIMIT)


def _sigmoid(x):
    return 1.0 / (1.0 + jnp.exp(-x))


def _silu(x):
    return x * _sigmoid(x)


def _adaln(x, nw, shift, scale):
    ms = jnp.mean(x * x, axis=-1, keepdims=True)
    return x * lax.rsqrt(ms + EPS) * (nw * (1.0 + scale)) + shift


def _head_norm(o, gain):
    ms = jnp.mean(o * o, axis=-1, keepdims=True)
    return o * lax.rsqrt(ms + EPS) * gain


def _split2(x):
    hi = x.astype(BF16)
    lo = (x - hi.astype(F32)).astype(BF16)
    return hi, lo


def _mod_kernel(c_ref, w_ref, b_ref, o_ref):
    a = _silu(c_ref[...]).astype(BF16)
    o_ref[...] = jnp.dot(a, w_ref[...].astype(BF16), preferred_element_type=F32) + b_ref[...]


def _modulation(c_rows, w, b, tn):
    d, n = w.shape
    return pl.pallas_call(
        _mod_kernel,
        grid=(n // tn,),
        in_specs=[
            pl.BlockSpec((MOD_ROWS, d), lambda j: (0, 0)),
            pl.BlockSpec((d, tn), lambda j: (0, j)),
            pl.BlockSpec((1, tn), lambda j: (0, j)),
        ],
        out_specs=pl.BlockSpec((MOD_ROWS, tn), lambda j: (0, j)),
        out_shape=jax.ShapeDtypeStruct((MOD_ROWS, n), F32),
        compiler_params=_params(("arbitrary",)),
        name="modulation",
    )(c_rows, w, b.reshape(1, n))


def _ffn_kernel(x_ref, nw_ref, sh_ref, sc_ref, ga_ref, wg_ref, wu_ref, wd_ref,
                nw2_ref, sh2_ref, sc2_ref, *rest, final, emit):
    n_out = (1 if final else 2) + (3 if emit else 0)
    outs, (h_scr, acc_scr) = rest[:n_out], rest[n_out:]
    y_ref = outs[0] if final else outs[1]
    f = pl.program_id(1)

    @pl.when(f == 0)
    def _():
        h = _adaln(x_ref[...], nw_ref[...], sh_ref[...], sc_ref[...])
        h_scr[...] = h.astype(BF16)
        acc_scr[...] = jnp.zeros_like(acc_scr)

    wg, wu, wd = wg_ref[...], wu_ref[...], wd_ref[...]
    if emit:
        wg, wu, wd = wg.astype(BF16), wu.astype(BF16), wd.astype(BF16)
        outs[-3][...], outs[-2][...], outs[-1][...] = wg, wu, wd
    h = h_scr[...]
    g = jnp.dot(h, wg, preferred_element_type=F32)
    u = jnp.dot(h, wu, preferred_element_type=F32)
    a = (_silu(g) * u).astype(BF16)
    acc_scr[...] += jnp.dot(a, wd, preferred_element_type=F32)

    @pl.when(f == pl.num_programs(1) - 1)
    def _():
        xn = x_ref[...] + 0.5 * ga_ref[...] * acc_scr[...]
        y = _adaln(xn, nw2_ref[...], sh2_ref[...], sc2_ref[...])
        if not final:
            outs[0][...] = xn
        y_ref[...] = y.astype(y_ref.dtype)


def _ffn(x, nw, mods, wg, wu, wd, nw2, mods2, mod_spec, tm, tf, final, emit):
    n, d = x.shape
    nf = wg.shape[1]
    assert not emit or n == tm
    row = pl.BlockSpec((tm, d), lambda i, f: (i, 0))
    vec = pl.BlockSpec((1, d), lambda i, f: (0, 0))
    w_specs = [pl.BlockSpec((d, tf), lambda i, f: (0, f)),
               pl.BlockSpec((d, tf), lambda i, f: (0, f)),
               pl.BlockSpec((tf, d), lambda i, f: (f, 0))]
    in_specs = [row, vec, mod_spec, mod_spec, mod_spec, *w_specs, vec, mod_spec, mod_spec]
    out_specs = [row] if final else [row, row]
    out_shape = ([jax.ShapeDtypeStruct((n, d), F32)] if final else
                 [jax.ShapeDtypeStruct((n, d), F32), jax.ShapeDtypeStruct((n, d), BF16)])
    if emit:
        out_specs += w_specs
        out_shape += [jax.ShapeDtypeStruct(w.shape, BF16) for w in (wg, wu, wd)]
    return pl.pallas_call(
        functools.partial(_ffn_kernel, final=final, emit=emit),
        grid=(n // tm, nf // tf),
        in_specs=in_specs,
        out_specs=out_specs,
        out_shape=out_shape,
        scratch_shapes=[pltpu.VMEM((tm, d), BF16), pltpu.VMEM((tm, d), F32)],
        compiler_params=_params(("arbitrary", "arbitrary")),
        name="ffn_final" if final else "ffn",
    )(x, nw.reshape(1, d), mods[0], mods[1], mods[2], wg, wu, wd,
      nw2.reshape(1, d), mods2[0], mods2[1])


def _in_proj_kernel(h_ref, w_ref, o_ref, wb_scr):
    @pl.when(pl.program_id(1) == 0)
    def _():
        wb_scr[...] = w_ref[...].astype(BF16)

    o_ref[...] = jnp.dot(h_ref[...], wb_scr[...], preferred_element_type=F32)


def _in_proj(h, w, col0, ncols, tm, tn):
    n, d = h.shape
    off = col0 // tn
    return pl.pallas_call(
        _in_proj_kernel,
        grid=(ncols // tn, n // tm),
        in_specs=[
            pl.BlockSpec((tm, d), lambda j, i: (i, 0)),
            pl.BlockSpec((d, tn), lambda j, i: (0, j + off)),
        ],
        out_specs=pl.BlockSpec((tm, tn), lambda j, i: (i, j)),
        out_shape=jax.ShapeDtypeStruct((n, ncols), F32),
        scratch_shapes=[pltpu.VMEM((d, tn), BF16)],
        compiler_params=_params(("arbitrary", "arbitrary")),
        name="in_proj",
    )(h, w)


def _out_proj_kernel(oa_ref, ob_ref, w_ref, x_ref, ga_ref, o_ref):
    m = jnp.dot(oa_ref[...].astype(BF16), w_ref[:D_GROUP, :], preferred_element_type=F32)
    m += jnp.dot(ob_ref[...].astype(BF16), w_ref[D_GROUP:, :], preferred_element_type=F32)
    o_ref[...] = x_ref[...] + ga_ref[...] * m


def _out_proj(oa, ob, w, x, gate, mod_spec2, tm):
    n, d = x.shape
    half = pl.BlockSpec((tm, D_GROUP), lambda i: (i, 0))
    row = pl.BlockSpec((tm, d), lambda i: (i, 0))
    return pl.pallas_call(
        _out_proj_kernel,
        grid=(n // tm,),
        in_specs=[half, half, pl.BlockSpec((2 * D_GROUP, d), lambda i: (0, 0)), row, mod_spec2],
        out_specs=row,
        out_shape=jax.ShapeDtypeStruct((n, d), F32),
        compiler_params=_params(("arbitrary",)),
        name="out_proj",
    )(oa, ob, w, x, gate)


def _gla_kernel(q_ref, f_ref, i_ref, g_ref, lbl_ref, gain_ref, s0_ref, o_ref, s_ref, st_scr,
                *, chunk, n_valid, n_groups):
    t = pl.program_id(1)
    n_chunks = LANE // chunk
    shift = chunk.bit_length() - 1

    @pl.when(t == 0)
    def _():
        for h in range(N_HEADS):
            st_scr[h] = s0_ref[h].T

    r_id = lax.broadcasted_iota(jnp.int32, (LANE, LANE), 0)
    c_id = lax.broadcasted_iota(jnp.int32, (LANE, LANE), 1)
    same = (r_id >> shift) == (c_id >> shift)
    causal = same & (c_id <= r_id)
    m_cum = jnp.concatenate([jnp.where(causal, 1.0, 0.0), jnp.where(same, 1.0, 0.0)],
                            axis=0).astype(BF16)

    heads = [slice(h * HEAD_DIM, (h + 1) * HEAD_DIM) for h in range(N_HEADS)]
    lbl = lbl_ref[...]
    mx = jnp.maximum(lbl, 0.0)
    e1 = jnp.exp(lbl - mx)
    lb = e1 / (e1 + jnp.exp(-mx))

    rows = [slice(g * LANE, (g + 1) * LANE) for g in range(n_groups)]
    kk, bb, v, vb, qd, k_inv, k_end, decay, o, v_t = ([None] * n_groups for _ in range(10))
    st = [st_scr[h] for h in range(N_HEADS)]

    def gates(g):
        f = lb + (1.0 - lb) * _sigmoid(f_ref[rows[g], :])
        logf = jnp.log(f)
        k = 1.0 - f
        if n_valid < LANE:
            valid = lax.broadcasted_iota(jnp.int32, (LANE, D_GROUP), 0) < n_valid
            logf = jnp.where(valid, logf, 0.0)
            k = jnp.where(valid, k, 0.0)
        hi = logf.astype(BF16)
        r1 = logf - hi.astype(F32)
        mid = r1.astype(BF16)
        lo = (r1 - mid.astype(F32)).astype(BF16)
        kk[g] = k
        bb[g] = jnp.dot(m_cum, jnp.concatenate([hi, mid, lo], axis=1),
                        preferred_element_type=F32)

    def decays(g):
        s = bb[g][:, :D_GROUP] + bb[g][:, D_GROUP:2 * D_GROUP] + bb[g][:, 2 * D_GROUP:]
        b = s[:LANE]
        b_last = s[LANE:]
        v[g] = i_ref[rows[g], :]
        vb[g] = v[g].astype(BF16)
        qd[g] = (q_ref[rows[g], :] * ATT_SCALE * jnp.exp(b)).astype(BF16)
        k_inv[g] = (kk[g] * jnp.exp(-b)).astype(BF16)
        k_end[g] = (kk[g] * jnp.exp(b_last - b)).astype(BF16)
        decay[g] = jnp.exp(b_last)

    def intra(g):
        a = [lax.dot_general(qd[g][:, c], k_inv[g][:, c], NT_DIMS, preferred_element_type=F32)
             for c in heads]
        a = [jnp.where(causal, x, 0.0).astype(BF16) for x in a]
        o[g] = [jnp.dot(x, vb[g][:, c], preferred_element_type=F32) for x, c in zip(a, heads)]
        v_t[g] = [[v[g][j * chunk:(j + 1) * chunk, c].T.astype(BF16) for j in range(n_chunks)]
                  for c in heads]

    def state(g):
        o_state = [[] for _ in heads]
        for j in range(n_chunks):
            for h, c in enumerate(heads):
                o_state[h].append(lax.dot_general(qd[g][j * chunk:(j + 1) * chunk, c],
                                                  st[h].astype(BF16), NT_DIMS,
                                                  preferred_element_type=F32))
                upd = jnp.dot(v_t[g][h][j], k_end[g][j * chunk:(j + 1) * chunk, c],
                              preferred_element_type=F32)
                st[h] = decay[g][j * chunk:j * chunk + 1, c] * st[h] + upd
        for h in range(N_HEADS):
            o[g][h] = o[g][h] + (jnp.concatenate(o_state[h], axis=0) if n_chunks > 1
                                 else o_state[h][0])

    def emit(g):
        for h, c in enumerate(heads):
            o_ref[rows[g], c] = (_head_norm(o[g][h], gain_ref[:, c])
                                 * _silu(g_ref[rows[g], c])).astype(o_ref.dtype)

    _skewed(n_groups, [gates, decays, intra, state, emit])
    for h in range(N_HEADS):
        st_scr[h] = st[h]

    @pl.when(t == pl.num_programs(1) - 1)
    def _():
        for h in range(N_HEADS):
            s_ref[h] = st_scr[h].T


def _gla(pa, lb_logits, gain, s0, n_seq, rows, chunk, n_valid, out_dtype):
    n_groups = min(GLA_GROUPS, rows // LANE)
    tile = n_groups * LANE
    n_tiles = rows // tile

    def col(group):
        return pl.BlockSpec((tile, D_GROUP), lambda b, t: (b * n_tiles + t, group))
    vec = pl.BlockSpec((1, D_GROUP), lambda b, t: (0, 0))
    state = pl.BlockSpec((None, N_HEADS, HEAD_DIM, HEAD_DIM), lambda b, t: (b, 0, 0, 0))
    return pl.pallas_call(
        functools.partial(_gla_kernel, chunk=chunk, n_valid=n_valid, n_groups=n_groups),
        grid=(n_seq, n_tiles),
        in_specs=[col(0), col(1), col(2), col(3), vec, vec, state],
        out_specs=(pl.BlockSpec((tile, D_GROUP), lambda b, t: (b * n_tiles + t, 0)), state),
        out_shape=(jax.ShapeDtypeStruct((n_seq * rows, D_GROUP), out_dtype),
                   jax.ShapeDtypeStruct((n_seq, N_HEADS, HEAD_DIM, HEAD_DIM), F32)),
        scratch_shapes=[pltpu.VMEM((N_HEADS, HEAD_DIM, HEAD_DIM), F32)],
        compiler_params=_params(("arbitrary", "arbitrary")),
        name="hgrn2",
    )(pa, pa, pa, pa, lb_logits.reshape(1, D_GROUP), gain.reshape(1, D_GROUP), s0)


def _suffix_weights():
    r_id = lax.broadcasted_iota(jnp.int32, (2 * LANE, 2 * LANE), 0) & (LANE - 1)
    c_id = lax.broadcasted_iota(jnp.int32, (2 * LANE, 2 * LANE), 1)
    return jnp.where((r_id >= c_id) | (c_id >= LANE), 1.0, 0.0).astype(BF16)


def _softplus2(z):
    return jnp.maximum(z, 0.0) + jnp.log(1.0 + jnp.exp2(-jnp.abs(z))) * LOG2E


def _sb_softplus(z_blocks):
    sp = [_softplus2(z) for z in z_blocks]
    hi, lo = _split2(sp[0] if len(sp) == 1 else jnp.concatenate(sp, axis=0))
    return jnp.concatenate([hi, lo], axis=1)


def _sb_weights(z_blocks, rt, run):
    m = z_blocks[0].shape[0]
    out = []
    for n, z in enumerate(z_blocks):
        within = rt[n * m:(n + 1) * m, :LANE]
        total = rt[n * m:(n + 1) * m, LANE:]
        out.append(jnp.exp2(z - within if run is None else z - within - run))
        run = total if run is None else run + total
    return out, run


def _skewed(n_items, stages):
    for tick in range(n_items + len(stages) - 1):
        for s in range(len(stages) - 1, -1, -1):
            i = tick - s
            if 0 <= i < n_items:
                stages[s](i)


def _sbp_kernel(q_ref, k_ref, v_ref, bias_ref, gain_ref, o_ref, q_scr, acc_scr, run_scr):
    i = pl.program_id(1)
    w2 = _suffix_weights()
    q_scr[...] = (q_ref[...] * (ATT_SCALE * LOG2E)).astype(BF16)
    acc_scr[...] = jnp.zeros_like(acc_scr)
    run_scr[...] = jnp.zeros_like(run_scr)
    r_id = lax.broadcasted_iota(jnp.int32, (SB_TILE, LANE), 0)
    c_id = lax.broadcasted_iota(jnp.int32, (SB_TILE, LANE), 1)
    n_blk = SB_TILE // LANE

    def visit(slabs, diagonal):
        n = len(slabs) * N_HEADS
        rows = [pl.ds(pl.multiple_of(j * SB_TILE, SB_TILE), SB_TILE) for j in slabs]
        order = range(n_blk - 1, -1, -1)
        if diagonal:
            mask = [jnp.where(c_id + m * LANE < r_id, 0.0, MASKED) for m in order]
        heads = [slice(h * HEAD_DIM, (h + 1) * HEAD_DIM) for h in range(N_HEADS)]
        zb, hl, rt = [None] * n, [None] * n, [None] * n

        def scores(t):
            r, c = rows[t // N_HEADS], heads[t % N_HEADS]
            z = lax.dot_general(q_scr[:, c], k_ref[r, c].astype(BF16), NT_DIMS,
                                preferred_element_type=F32) + bias_ref[t % N_HEADS]
            zb[t] = [z[:, m * LANE:(m + 1) * LANE] for m in order]
            if diagonal:
                zb[t] = [z_m + m_m for z_m, m_m in zip(zb[t], mask)]

        def softplus(t):
            hl[t] = _sb_softplus(zb[t])

        def suffix(t):
            rt[t] = jnp.dot(hl[t], w2, preferred_element_type=F32)

        def accumulate(t):
            r, h = rows[t // N_HEADS], t % N_HEADS
            a_blocks, total = _sb_weights(zb[t], rt[t], None)
            a = jnp.concatenate(a_blocks[::-1], axis=1).astype(BF16)
            p = jnp.dot(a, v_ref[r, heads[h]].astype(BF16), preferred_element_type=F32)
            run = run_scr[h]
            acc_scr[h] += p * jnp.exp2(-run)
            run_scr[h] = run + total

        _skewed(n, [scores, softplus, suffix, accumulate])

    visit([i], True)

    def pair(t, carry):
        visit([i - 1 - 2 * t, i - 2 - 2 * t], False)
        return carry
    lax.fori_loop(0, i // 2, pair, 0)

    @pl.when(i % 2 == 1)
    def _():
        visit([0], False)

    for h in range(N_HEADS):
        c = slice(h * HEAD_DIM, (h + 1) * HEAD_DIM)
        o_ref[:, c] = _head_norm(acc_scr[h], gain_ref[:, c]).astype(o_ref.dtype)


def _sb_prompt(pq, q_col0, k, v, bias_rows, gain, n_seq, seq):
    nq = seq // SB_TILE
    qoff = q_col0 // D_GROUP
    kv = pl.BlockSpec((seq, D_GROUP), lambda b, i: (b, 0))
    return pl.pallas_call(
        _sbp_kernel,
        grid=(n_seq, nq),
        in_specs=[
            pl.BlockSpec((SB_TILE, D_GROUP), lambda b, i: (b * nq + i, qoff)),
            kv, kv,
            pl.BlockSpec((N_HEADS, 1, SB_TILE), lambda b, i: (0, 0, 0)),
            pl.BlockSpec((1, D_GROUP), lambda b, i: (0, 0)),
        ],
        out_specs=pl.BlockSpec((SB_TILE, D_GROUP), lambda b, i: (b * nq + i, 0)),
        out_shape=jax.ShapeDtypeStruct((n_seq * seq, D_GROUP), BF16),
        scratch_shapes=[pltpu.VMEM((SB_TILE, D_GROUP), BF16),
                        pltpu.VMEM((N_HEADS, SB_TILE, HEAD_DIM), F32),
                        pltpu.VMEM((N_HEADS, SB_TILE, LANE), F32)],
        compiler_params=_params(("arbitrary", "arbitrary")),
        name="stickbreak_prompt",
    )(pq, k, v, bias_rows, gain.reshape(1, D_GROUP))


def _sbs_kernel(pt_ref, q_ref, kn_ref, vn_ref, *rest, n_new):
    del pt_ref
    kc_refs = rest[:SB_PAGES]
    vc_refs = rest[SB_PAGES:2 * SB_PAGES]
    bias_ref, gain_ref, o_ref, q_scr, acc_scr, run_scr = rest[2 * SB_PAGES:]
    j = pl.program_id(1)
    nr = N_HEADS * n_new
    w2 = _suffix_weights()
    row = lax.broadcasted_iota(jnp.int32, (nr, LANE), 0)
    col = lax.broadcasted_iota(jnp.int32, (nr, LANE), 1)
    own = (col & (N_HEADS - 1)) == (row >> (n_new.bit_length() - 1))

    def visit(k_rows, v_rows, bias):
        n = len(k_rows)
        q = q_scr[...]
        zb, hl, rt, a = [None] * n, [None] * n, [None] * n, [None] * n
        state = {"run": run_scr[...], "acc": acc_scr[...]}

        def scores(p):
            z = lax.dot_general(q, k_rows[p](), NT_DIMS, preferred_element_type=F32)
            zb[p] = [z[:, m * LANE:(m + 1) * LANE] + bias
                     for m in range(z.shape[1] // LANE - 1, -1, -1)]

        def softplus(p):
            hl[p] = _sb_softplus(zb[p])

        def suffix(p):
            rt[p] = jnp.dot(hl[p], w2, preferred_element_type=F32)

        def weights(p):
            blocks, state["run"] = _sb_weights(zb[p], rt[p], state["run"])
            blocks = blocks[::-1]
            a[p] = (blocks[0] if len(blocks) == 1
                    else jnp.concatenate(blocks, axis=1)).astype(BF16)

        def accumulate(p):
            state["acc"] = state["acc"] + jnp.dot(a[p], v_rows[p](), preferred_element_type=F32)

        _skewed(n, [scores, softplus, suffix, weights, accumulate])
        acc_scr[...] = state["acc"]
        run_scr[...] = state["run"]

    @pl.when(j == 0)
    def _():
        q = q_ref[...] * (ATT_SCALE * LOG2E)
        q_scr[...] = jnp.concatenate(
            [q[:, h * HEAD_DIM:(h + 1) * HEAD_DIM] for h in range(N_HEADS)], axis=0).astype(BF16)
        run_scr[...] = jnp.zeros_like(run_scr)
        acc_scr[...] = jnp.zeros_like(acc_scr)
        pad = jnp.zeros((LANE - nr, HEAD_DIM), F32)
        kb = jnp.concatenate([kn_ref[...], pad], axis=0).astype(BF16)
        vb = jnp.concatenate([vn_ref[...], pad], axis=0).astype(BF16)
        earlier = (col >> (N_HEADS.bit_length() - 1)) < (row & (n_new - 1))
        visit([lambda: kb], [lambda: vb], jnp.where(own & earlier, bias_ref[...], MASKED))

    def page_rows(ref):
        return lambda: ref[...].reshape(PAGE * N_HEADS, HEAD_DIM).astype(BF16)
    visit([page_rows(r) for r in kc_refs], [page_rows(r) for r in vc_refs],
          jnp.where(own, bias_ref[...], MASKED))

    @pl.when(j == pl.num_programs(1) - 1)
    def _():
        acc = acc_scr[...]
        heads = [acc[h * n_new:(h + 1) * n_new] for h in range(N_HEADS)]
        heads = [o * lax.rsqrt(jnp.mean(o * o, axis=-1, keepdims=True) + EPS) for o in heads]
        o_ref[...] = jnp.concatenate(heads, axis=1) * gain_ref[...]


def _sb_sample(pq, q_col0, k_new, v_new, cache_k, cache_v, page_table, bias_rep, gain,
               n_seq, n_new):
    n_pages = page_table.shape[1]
    qoff = q_col0 // D_GROUP
    nr = N_HEADS * n_new
    assert nr <= LANE
    new = pl.BlockSpec((None, nr, HEAD_DIM), lambda b, j, pt: (b, 0, 0))

    def page(r):
        return pl.BlockSpec(
            (None, None, PAGE, N_HEADS, HEAD_DIM),
            lambda b, j, pt: (0, pt[b, n_pages - 1 - (j * SB_PAGES + r)], 0, 0, 0))
    pages = [page(r) for r in range(SB_PAGES)]
    grid_spec = pltpu.PrefetchScalarGridSpec(
        num_scalar_prefetch=1,
        grid=(n_seq, n_pages // SB_PAGES),
        in_specs=[pl.BlockSpec((n_new, D_GROUP), lambda b, j, pt: (b, qoff)), new, new,
                  *pages, *pages,
                  pl.BlockSpec((nr, LANE), lambda b, j, pt: (0, 0)),
                  pl.BlockSpec((1, D_GROUP), lambda b, j, pt: (0, 0))],
        out_specs=pl.BlockSpec((n_new, D_GROUP), lambda b, j, pt: (b, 0)),
        scratch_shapes=[pltpu.VMEM((nr, HEAD_DIM), BF16), pltpu.VMEM((nr, HEAD_DIM), F32),
                        pltpu.VMEM((nr, LANE), F32)],
    )
    return pl.pallas_call(
        functools.partial(_sbs_kernel, n_new=n_new),
        grid_spec=grid_spec,
        out_shape=jax.ShapeDtypeStruct((n_seq * n_new, D_GROUP), F32),
        compiler_params=_params(("arbitrary", "arbitrary")),
        name="stickbreak_sample",
    )(page_table, pq, k_new, v_new, *([cache_k] * SB_PAGES), *([cache_v] * SB_PAGES),
      bias_rep, gain.reshape(1, D_GROUP))


def kernel(x_prompt, x_sample, cache_k, cache_v, state_hgrn, page_table, c_prompt, c_sample,
           lb_logits, norm_ffn1, norm_mix, norm_ffn2, w_mod, b_mod,
           w_ffn1_gate, w_ffn1_up, w_ffn1_down, w_in, g_out_a, g_out_b, b_sb, w_out,
           w_ffn2_gate, w_ffn2_up, w_ffn2_down, norm_final, w_final_mod, b_final_mod):
    n_p, seq, d = x_prompt.shape
    n_s, n_new, _ = x_sample.shape
    assert w_mod.shape[0] == 1, "single-layer trunk"
    assert n_p + n_s <= MOD_ROWS
    assert n_new & (n_new - 1) == 0 and n_new <= PAGE
    assert page_table.shape[1] % SB_PAGES == 0 and seq % SB_TILE == 0

    c_rows = jnp.concatenate(
        [c_prompt, c_sample, jnp.zeros((MOD_ROWS - n_p - n_s, d), F32)], axis=0)
    mod = _modulation(c_rows, w_mod[0], b_mod[0], PROJ_COLS).reshape(MOD_ROWS, N_MOD, d)
    fmod = _modulation(c_rows, w_final_mod, b_final_mod, PROJ_COLS).reshape(MOD_ROWS, 2, d)

    w_out_b = w_out[0].astype(BF16)
    bias2 = b_sb[0] * LOG2E
    bias_rows = jnp.broadcast_to(bias2.reshape(N_HEADS, 1, 1), (N_HEADS, 1, SB_TILE))
    bias_rep = jnp.broadcast_to(jnp.repeat(bias2, n_new)[:, None], (N_HEADS * n_new, LANE))

    def layer(x, mods, fmods, mod_spec, mod_spec1, tm, tf, tm_proj, mixer, w_ffn1, w_ffn2, emit):
        x1, h2, *w1 = _ffn(x, norm_ffn1[0], mods[0:3], *w_ffn1, norm_mix[0], mods[3:5],
                           mod_spec, tm, tf, final=False, emit=emit)
        pa = _in_proj(h2, w_in[0], 0, 5 * D_GROUP, tm_proj, PROJ_COLS)
        k_new = _in_proj(h2, w_in[0], 5 * D_GROUP, D_GROUP, tm_proj, PROJ_COLS)
        v_new = _in_proj(h2, w_in[0], 6 * D_GROUP, D_GROUP, tm_proj, PROJ_COLS)
        oa, ob, s_fin = mixer(pa, k_new, v_new)
        x2 = _out_proj(oa, ob, w_out_b, x1, mods[5], mod_spec1, tm)
        y, *w2 = _ffn(x2, norm_ffn2[0], mods[6:9], *w_ffn2, norm_final, fmods,
                      mod_spec, tm, tf, final=True, emit=emit)
        return y, k_new, v_new, s_fin, w1, w2

    tiles_per_seq = seq // FFN_ROWS
    mods_p = [mod[:n_p, j].reshape(n_p, 1, d) for j in range(N_MOD)]
    fmods_p = [fmod[:n_p, j].reshape(n_p, 1, d) for j in range(2)]
    spec_p = pl.BlockSpec((None, 1, d), lambda i, f: (i // tiles_per_seq, 0, 0))
    spec_p1 = pl.BlockSpec((None, 1, d), lambda i: (i // tiles_per_seq, 0, 0))

    def mixer_p(pa, k_new, v_new):
        s0 = jnp.zeros((n_p, N_HEADS, HEAD_DIM, HEAD_DIM), F32)
        oa, s_fin = _gla(pa, lb_logits[0], g_out_a[0], s0, n_p, seq, GLA_CHUNK, LANE, BF16)
        ob = _sb_prompt(pa, 4 * D_GROUP, k_new, v_new, bias_rows, g_out_b[0], n_p, seq)
        return oa, ob, s_fin

    rows_s = n_s * n_new
    mods_s = [jnp.repeat(mod[n_p:n_p + n_s, j], n_new, axis=0).reshape(1, rows_s, d)
              for j in range(N_MOD)]
    fmods_s = [jnp.repeat(fmod[n_p:n_p + n_s, j], n_new, axis=0).reshape(1, rows_s, d)
               for j in range(2)]
    spec_s = pl.BlockSpec((None, rows_s, d), lambda i, f: (0, 0, 0))
    spec_s1 = pl.BlockSpec((None, rows_s, d), lambda i: (0, 0, 0))

    def mixer_s(pa, k_new, v_new):
        pa_pad = jnp.pad(pa[:, :4 * D_GROUP].reshape(n_s, n_new, 4 * D_GROUP),
                         ((0, 0), (0, LANE - n_new), (0, 0))).reshape(n_s * LANE, 4 * D_GROUP)
        oa_pad, s_fin = _gla(pa_pad, lb_logits[0], g_out_a[0], state_hgrn[0], n_s, LANE, LANE,
                             n_new, F32)
        oa = oa_pad.reshape(n_s, LANE, D_GROUP)[:, :n_new].reshape(rows_s, D_GROUP)
        rows_th = (n_s, n_new * N_HEADS, HEAD_DIM)
        ob = _sb_sample(pa, 4 * D_GROUP, k_new.reshape(rows_th), v_new.reshape(rows_th),
                        cache_k, cache_v, page_table, bias_rep, g_out_b[0], n_s, n_new)
        return oa, ob, s_fin

    y_s, k_s, v_s, s_s, w1, w2 = layer(
        x_sample.reshape(rows_s, d), mods_s, fmods_s, spec_s, spec_s1, rows_s, FFN_COLS, rows_s,
        mixer_s, (w_ffn1_gate[0], w_ffn1_up[0], w_ffn1_down[0]),
        (w_ffn2_gate[0], w_ffn2_up[0], w_ffn2_down[0]), True)
    y_p, k_p, v_p, s_p, _, _ = layer(
        x_prompt.reshape(n_p * seq, d), mods_p, fmods_p, spec_p, spec_p1, FFN_ROWS, FFN_COLS,
        PROJ_ROWS, mixer_p, w1, w2, False)

    return (y_p.reshape(n_p, seq, d), y_s.reshape(n_s, n_new, d),
            k_p.reshape(1, n_p, seq, N_HEADS, HEAD_DIM), v_p.reshape(1, n_p, seq, N_HEADS, HEAD_DIM),
            k_s.reshape(1, n_s, n_new, N_HEADS, HEAD_DIM), v_s.reshape(1, n_s, n_new, N_HEADS, HEAD_DIM),
            s_p[None], s_s[None])
```

```python
import functools

import jax
import jax.numpy as jnp
from jax import lax
from jax.experimental import pallas as pl
from jax.experimental.pallas import tpu as pltpu

F32 = jnp.float32
BF16 = jnp.bfloat16

N_HEADS = 8
HEAD_DIM = 128
D_GROUP = N_HEADS * HEAD_DIM
N_MOD = 9
GLA_CHUNK = 32
GLA_GROUPS = 8
PAGE = 128
EPS = 1e-6
ATT_SCALE = HEAD_DIM ** -0.5
LOG2E = 1.4426950408889634
MASKED = -1e30
MOD_ROWS = 16
LANE = 128
SB_TILE = 256
SB_PAGES = 16
FFN_ROWS = 512
FFN_COLS = 512
PROJ_ROWS = 1024
PROJ_COLS = 1024
VMEM_LIMIT = 56 * 1024 * 1024

NT_DIMS = (((1,), (1,)), ((), ()))


def _params(sem):
    return pltpu.CompilerParams(dimension_semantics=sem, vmem_limit_bytes=VMEM_LIMIT)


def _sigmoid(x):
    return 1.0 / (1.0 + jnp.exp(-x))


def _silu(x):
    return x * _sigmoid(x)


def _adaln(x, nw, shift, scale):
    ms = jnp.mean(x * x, axis=-1, keepdims=True)
    return x * lax.rsqrt(ms + EPS) * (nw * (1.0 + scale)) + shift


def _head_norm(o, gain):
    ms = jnp.mean(o * o, axis=-1, keepdims=True)
    return o * lax.rsqrt(ms + EPS) * gain


def _split2(x):
    hi = x.astype(BF16)
    lo = (x - hi.astype(F32)).astype(BF16)
    return hi, lo


def _mod_kernel(c_ref, w_ref, b_ref, o_ref):
    a = _silu(c_ref[...]).astype(BF16)
    o_ref[...] = jnp.dot(a, w_ref[...].astype(BF16), preferred_element_type=F32) + b_ref[...]


def _modulation(c_rows, w, b, tn):
    d, n = w.shape
    return pl.pallas_call(
        _mod_kernel,
        grid=(n // tn,),
        in_specs=[
            pl.BlockSpec((MOD_ROWS, d), lambda j: (0, 0)),
            pl.BlockSpec((d, tn), lambda j: (0, j)),
            pl.BlockSpec((1, tn), lambda j: (0, j)),
        ],
        out_specs=pl.BlockSpec((MOD_ROWS, tn), lambda j: (0, j)),
        out_shape=jax.ShapeDtypeStruct((MOD_ROWS, n), F32),
        compiler_params=_params(("arbitrary",)),
        name="modulation",
    )(c_rows, w, b.reshape(1, n))


def _ffn_kernel(x_ref, nw_ref, sh_ref, sc_ref, ga_ref, wg_ref, wu_ref, wd_ref,
                nw2_ref, sh2_ref, sc2_ref, *rest, final, emit):
    n_out = (1 if final else 2) + (3 if emit else 0)
    outs, (h_scr, acc_scr) = rest[:n_out], rest[n_out:]
    y_ref = outs[0] if final else outs[1]
    f = pl.program_id(1)

    @pl.when(f == 0)
    def _():
        h = _adaln(x_ref[...], nw_ref[...], sh_ref[...], sc_ref[...])
        h_scr[...] = h.astype(BF16)
        acc_scr[...] = jnp.zeros_like(acc_scr)

    wg, wu, wd = wg_ref[...], wu_ref[...], wd_ref[...]
    if emit:
        wg, wu, wd = wg.astype(BF16), wu.astype(BF16), wd.astype(BF16)
        outs[-3][...], outs[-2][...], outs[-1][...] = wg, wu, wd
    h = h_scr[...]
    g = jnp.dot(h, wg, preferred_element_type=F32)
    u = jnp.dot(h, wu, preferred_element_type=F32)
    a = (_silu(g) * u).astype(BF16)
    acc_scr[...] += jnp.dot(a, wd, preferred_element_type=F32)

    @pl.when(f == pl.num_programs(1) - 1)
    def _():
        xn = x_ref[...] + 0.5 * ga_ref[...] * acc_scr[...]
        y = _adaln(xn, nw2_ref[...], sh2_ref[...], sc2_ref[...])
        if not final:
            outs[0][...] = xn
        y_ref[...] = y.astype(y_ref.dtype)


def _ffn(x, nw, mods, wg, wu, wd, nw2, mods2, mod_spec, tm, tf, final, emit):
    n, d = x.shape
    nf = wg.shape[1]
    assert not emit or n == tm
    row = pl.BlockSpec((tm, d), lambda i, f: (i, 0))
    vec = pl.BlockSpec((1, d), lambda i, f: (0, 0))
    w_specs = [pl.BlockSpec((d, tf), lambda i, f: (0, f)),
               pl.BlockSpec((d, tf), lambda i, f: (0, f)),
               pl.BlockSpec((tf, d), lambda i, f: (f, 0))]
    in_specs = [row, vec, mod_spec, mod_spec, mod_spec, *w_specs, vec, mod_spec, mod_spec]
    out_specs = [row] if final else [row, row]
    out_shape = ([jax.ShapeDtypeStruct((n, d), F32)] if final else
                 [jax.ShapeDtypeStruct((n, d), F32), jax.ShapeDtypeStruct((n, d), BF16)])
    if emit:
        out_specs += w_specs
        out_shape += [jax.ShapeDtypeStruct(w.shape, BF16) for w in (wg, wu, wd)]
    return pl.pallas_call(
        functools.partial(_ffn_kernel, final=final, emit=emit),
        grid=(n // tm, nf // tf),
        in_specs=in_specs,
        out_specs=out_specs,
        out_shape=out_shape,
        scratch_shapes=[pltpu.VMEM((tm, d), BF16), pltpu.VMEM((tm, d), F32)],
        compiler_params=_params(("arbitrary", "arbitrary")),
        name="ffn_final" if final else "ffn",
    )(x, nw.reshape(1, d), mods[0], mods[1], mods[2], wg, wu, wd,
      nw2.reshape(1, d), mods2[0], mods2[1])


def _in_proj_kernel(h_ref, w_ref, o_ref, wb_scr):
    @pl.when(pl.program_id(1) == 0)
    def _():
        wb_scr[...] = w_ref[...].astype(BF16)

    o_ref[...] = jnp.dot(h_ref[...], wb_scr[...], preferred_element_type=F32)


def _in_proj(h, w, col0, ncols, tm, tn):
    n, d = h.shape
    off = col0 // tn
    return pl.pallas_call(
        _in_proj_kernel,
        grid=(ncols // tn, n // tm),
        in_specs=[
            pl.BlockSpec((tm, d), lambda j, i: (i, 0)),
            pl.BlockSpec((d, tn), lambda j, i: (0, j + off)),
        ],
        out_specs=pl.BlockSpec((tm, tn), lambda j, i: (i, j)),
        out_shape=jax.ShapeDtypeStruct((n, ncols), F32),
        scratch_shapes=[pltpu.VMEM((d, tn), BF16)],
        compiler_params=_params(("arbitrary", "arbitrary")),
        name="in_proj",
    )(h, w)


def _out_proj_kernel(oa_ref, ob_ref, w_ref, x_ref, ga_ref, o_ref):
    m = jnp.dot(oa_ref[...].astype(BF16), w_ref[:D_GROUP, :], preferred_element_type=F32)
    m += jnp.dot(ob_ref[...].astype(BF16), w_ref[D_GROUP:, :], preferred_element_type=F32)
    o_ref[...] = x_ref[...] + ga_ref[...] * m


def _out_proj(oa, ob, w, x, gate, mod_spec2, tm):
    n, d = x.shape
    half = pl.BlockSpec((tm, D_GROUP), lambda i: (i, 0))
    row = pl.BlockSpec((tm, d), lambda i: (i, 0))
    return pl.pallas_call(
        _out_proj_kernel,
        grid=(n // tm,),
        in_specs=[half, half, pl.BlockSpec((2 * D_GROUP, d), lambda i: (0, 0)), row, mod_spec2],
        out_specs=row,
        out_shape=jax.ShapeDtypeStruct((n, d), F32),
        compiler_params=_params(("arbitrary",)),
        name="out_proj",
    )(oa, ob, w, x, gate)


def _gla_kernel(q_ref, f_ref, i_ref, g_ref, lbl_ref, gain_ref, s0_ref, o_ref, s_ref, st_scr,
                *, chunk, n_valid, n_groups):
    t = pl.program_id(1)
    n_chunks = LANE // chunk
    shift = chunk.bit_length() - 1

    @pl.when(t == 0)
    def _():
        for h in range(N_HEADS):
            st_scr[h] = s0_ref[h].T

    r_id = lax.broadcasted_iota(jnp.int32, (LANE, LANE), 0)
    c_id = lax.broadcasted_iota(jnp.int32, (LANE, LANE), 1)
    same = (r_id >> shift) == (c_id >> shift)
    causal = same & (c_id <= r_id)
    m_cum = jnp.concatenate([jnp.where(causal, 1.0, 0.0), jnp.where(same, 1.0, 0.0)],
                            axis=0).astype(BF16)

    heads = [slice(h * HEAD_DIM, (h + 1) * HEAD_DIM) for h in range(N_HEADS)]
    lbl = lbl_ref[...]
    mx = jnp.maximum(lbl, 0.0)
    e1 = jnp.exp(lbl - mx)
    lb = e1 / (e1 + jnp.exp(-mx))

    rows = [slice(g * LANE, (g + 1) * LANE) for g in range(n_groups)]
    kk, bb, v, vb, qd, k_inv, k_end, decay, o, v_t = ([None] * n_groups for _ in range(10))
    st = [st_scr[h] for h in range(N_HEADS)]

    def load(ref, g):
        if n_valid < LANE:
            return jnp.concatenate([ref[...], jnp.zeros((LANE - n_valid, D_GROUP), F32)], axis=0)
        return ref[rows[g], :]

    def gates(g):
        f = lb + (1.0 - lb) * _sigmoid(load(f_ref, g))
        logf = jnp.log(f)
        k = 1.0 - f
        if n_valid < LANE:
            valid = lax.broadcasted_iota(jnp.int32, (LANE, D_GROUP), 0) < n_valid
            logf = jnp.where(valid, logf, 0.0)
            k = jnp.where(valid, k, 0.0)
        hi = logf.astype(BF16)
        r1 = logf - hi.astype(F32)
        mid = r1.astype(BF16)
        lo = (r1 - mid.astype(F32)).astype(BF16)
        kk[g] = k
        bb[g] = jnp.dot(m_cum, jnp.concatenate([hi, mid, lo], axis=1),
                        preferred_element_type=F32)

    def decays(g):
        s = bb[g][:, :D_GROUP] + bb[g][:, D_GROUP:2 * D_GROUP] + bb[g][:, 2 * D_GROUP:]
        b = s[:LANE]
        b_last = s[LANE:]
        v[g] = load(i_ref, g)
        vb[g] = v[g].astype(BF16)
        qd[g] = (load(q_ref, g) * ATT_SCALE * jnp.exp(b)).astype(BF16)
        k_inv[g] = (kk[g] * jnp.exp(-b)).astype(BF16)
        k_end[g] = (kk[g] * jnp.exp(b_last - b)).astype(BF16)
        decay[g] = jnp.exp(b_last)

    def intra(g):
        a = [lax.dot_general(qd[g][:, c], k_inv[g][:, c], NT_DIMS, preferred_element_type=F32)
             for c in heads]
        a = [jnp.where(causal, x, 0.0).astype(BF16) for x in a]
        o[g] = [jnp.dot(x, vb[g][:, c], preferred_element_type=F32) for x, c in zip(a, heads)]
        v_t[g] = [[v[g][j * chunk:(j + 1) * chunk, c].T.astype(BF16) for j in range(n_chunks)]
                  for c in heads]

    def state(g):
        o_state = [[] for _ in heads]
        for j in range(n_chunks):
            for h, c in enumerate(heads):
                o_state[h].append(lax.dot_general(qd[g][j * chunk:(j + 1) * chunk, c],
                                                  st[h].astype(BF16), NT_DIMS,
                                                  preferred_element_type=F32))
                upd = jnp.dot(v_t[g][h][j], k_end[g][j * chunk:(j + 1) * chunk, c],
                              preferred_element_type=F32)
                st[h] = decay[g][j * chunk:j * chunk + 1, c] * st[h] + upd
        for h in range(N_HEADS):
            o[g][h] = o[g][h] + (jnp.concatenate(o_state[h], axis=0) if n_chunks > 1
                                 else o_state[h][0])

    def emit(g):
        gate = _silu(load(g_ref, g))
        for h, c in enumerate(heads):
            out = (_head_norm(o[g][h], gain_ref[:, c]) * gate[:, c]).astype(o_ref.dtype)
            if n_valid < LANE:
                o_ref[:, c] = out[:n_valid]
            else:
                o_ref[rows[g], c] = out

    _skewed(n_groups, [gates, decays, intra, state, emit])
    for h in range(N_HEADS):
        st_scr[h] = st[h]

    @pl.when(t == pl.num_programs(1) - 1)
    def _():
        for h in range(N_HEADS):
            s_ref[h] = st_scr[h].T


def _gla(pa, lb_logits, gain, s0, n_seq, rows, chunk, n_valid, out_dtype):
    n_groups = min(GLA_GROUPS, rows // LANE)
    n_tiles = rows // (n_groups * LANE)
    tile = n_groups * LANE if n_valid == LANE else n_valid
    assert n_valid == LANE or (rows == LANE and n_valid % 8 == 0)

    def col(group):
        return pl.BlockSpec((tile, D_GROUP), lambda b, t: (b * n_tiles + t, group))
    vec = pl.BlockSpec((1, D_GROUP), lambda b, t: (0, 0))
    state = pl.BlockSpec((None, N_HEADS, HEAD_DIM, HEAD_DIM), lambda b, t: (b, 0, 0, 0))
    return pl.pallas_call(
        functools.partial(_gla_kernel, chunk=chunk, n_valid=n_valid, n_groups=n_groups),
        grid=(n_seq, n_tiles),
        in_specs=[col(0), col(1), col(2), col(3), vec, vec, state],
        out_specs=(pl.BlockSpec((tile, D_GROUP), lambda b, t: (b * n_tiles + t, 0)), state),
        out_shape=(jax.ShapeDtypeStruct((n_seq * n_tiles * tile, D_GROUP), out_dtype),
                   jax.ShapeDtypeStruct((n_seq, N_HEADS, HEAD_DIM, HEAD_DIM), F32)),
        scratch_shapes=[pltpu.VMEM((N_HEADS, HEAD_DIM, HEAD_DIM), F32)],
        compiler_params=_params(("arbitrary", "arbitrary")),
        name="hgrn2",
    )(pa, pa, pa, pa, lb_logits.reshape(1, D_GROUP), gain.reshape(1, D_GROUP), s0)


def _suffix_weights():
    r_id = lax.broadcasted_iota(jnp.int32, (2 * LANE, 2 * LANE), 0) & (LANE - 1)
    c_id = lax.broadcasted_iota(jnp.int32, (2 * LANE, 2 * LANE), 1)
    return jnp.where((r_id >= c_id) | (c_id >= LANE), 1.0, 0.0).astype(BF16)


def _softplus2(z):
    return jnp.maximum(z, 0.0) + jnp.log(1.0 + jnp.exp2(-jnp.abs(z))) * LOG2E


def _sb_softplus(z_blocks):
    sp = [_softplus2(z) for z in z_blocks]
    hi, lo = _split2(sp[0] if len(sp) == 1 else jnp.concatenate(sp, axis=0))
    return jnp.concatenate([hi, lo], axis=1)


def _sb_weights(z_blocks, rt, run):
    m = z_blocks[0].shape[0]
    out = []
    for n, z in enumerate(z_blocks):
        within = rt[n * m:(n + 1) * m, :LANE]
        total = rt[n * m:(n + 1) * m, LANE:]
        out.append(jnp.exp2(z - within if run is None else z - within - run))
        run = total if run is None else run + total
    return out, run


def _skewed(n_items, stages):
    for tick in range(n_items + len(stages) - 1):
        for s in range(len(stages) - 1, -1, -1):
            i = tick - s
            if 0 <= i < n_items:
                stages[s](i)


def _sbp_kernel(q_ref, k_ref, v_ref, bias_ref, gain_ref, o_ref, q_scr, acc_scr, run_scr):
    i = pl.program_id(1)
    w2 = _suffix_weights()
    q_scr[...] = (q_ref[...] * (ATT_SCALE * LOG2E)).astype(BF16)
    acc_scr[...] = jnp.zeros_like(acc_scr)
    run_scr[...] = jnp.zeros_like(run_scr)
    r_id = lax.broadcasted_iota(jnp.int32, (SB_TILE, LANE), 0)
    c_id = lax.broadcasted_iota(jnp.int32, (SB_TILE, LANE), 1)
    n_blk = SB_TILE // LANE

    def visit(slabs, diagonal):
        n = len(slabs) * N_HEADS
        rows = [pl.ds(pl.multiple_of(j * SB_TILE, SB_TILE), SB_TILE) for j in slabs]
        order = range(n_blk - 1, -1, -1)
        if diagonal:
            mask = [jnp.where(c_id + m * LANE < r_id, 0.0, MASKED) for m in order]
        heads = [slice(h * HEAD_DIM, (h + 1) * HEAD_DIM) for h in range(N_HEADS)]
        zb, hl, rt = [None] * n, [None] * n, [None] * n

        def scores(t):
            r, c = rows[t // N_HEADS], heads[t % N_HEADS]
            z = lax.dot_general(q_scr[:, c], k_ref[r, c].astype(BF16), NT_DIMS,
                                preferred_element_type=F32) + bias_ref[t % N_HEADS]
            zb[t] = [z[:, m * LANE:(m + 1) * LANE] for m in order]
            if diagonal and t < N_HEADS:
                zb[t] = [z_m + m_m for z_m, m_m in zip(zb[t], mask)]

        def softplus(t):
            hl[t] = _sb_softplus(zb[t])

        def suffix(t):
            rt[t] = jnp.dot(hl[t], w2, preferred_element_type=F32)

        def accumulate(t):
            r, h = rows[t // N_HEADS], t % N_HEADS
            a_blocks, total = _sb_weights(zb[t], rt[t], None)
            a = jnp.concatenate(a_blocks[::-1], axis=1).astype(BF16)
            p = jnp.dot(a, v_ref[r, heads[h]].astype(BF16), preferred_element_type=F32)
            run = run_scr[h]
            acc_scr[h] += p * jnp.exp2(-run)
            run_scr[h] = run + total

        _skewed(n, [scores, softplus, suffix, accumulate])

    @pl.when(i == 0)
    def _():
        visit([0], True)

    @pl.when(i > 0)
    def _():
        visit([i, i - 1], True)

        def pair(t, carry):
            visit([i - 2 - 2 * t, i - 3 - 2 * t], False)
            return carry
        lax.fori_loop(0, (i - 1) // 2, pair, 0)

        @pl.when(i % 2 == 0)
        def _():
            visit([0], False)

    for h in range(N_HEADS):
        c = slice(h * HEAD_DIM, (h + 1) * HEAD_DIM)
        o_ref[:, c] = _head_norm(acc_scr[h], gain_ref[:, c]).astype(o_ref.dtype)


def _sb_prompt(pq, q_col0, k, v, bias_rows, gain, n_seq, seq):
    nq = seq // SB_TILE
    qoff = q_col0 // D_GROUP
    kv = pl.BlockSpec((seq, D_GROUP), lambda b, i: (b, 0))
    return pl.pallas_call(
        _sbp_kernel,
        grid=(n_seq, nq),
        in_specs=[
            pl.BlockSpec((SB_TILE, D_GROUP), lambda b, i: (b * nq + i, qoff)),
            kv, kv,
            pl.BlockSpec((N_HEADS, 1, SB_TILE), lambda b, i: (0, 0, 0)),
            pl.BlockSpec((1, D_GROUP), lambda b, i: (0, 0)),
        ],
        out_specs=pl.BlockSpec((SB_TILE, D_GROUP), lambda b, i: (b * nq + i, 0)),
        out_shape=jax.ShapeDtypeStruct((n_seq * seq, D_GROUP), BF16),
        scratch_shapes=[pltpu.VMEM((SB_TILE, D_GROUP), BF16),
                        pltpu.VMEM((N_HEADS, SB_TILE, HEAD_DIM), F32),
                        pltpu.VMEM((N_HEADS, SB_TILE, LANE), F32)],
        compiler_params=_params(("arbitrary", "arbitrary")),
        name="stickbreak_prompt",
    )(pq, k, v, bias_rows, gain.reshape(1, D_GROUP))


def _sbs_kernel(pt_ref, q_ref, kn_ref, vn_ref, *rest, n_new):
    del pt_ref
    kc_refs = rest[:SB_PAGES]
    vc_refs = rest[SB_PAGES:2 * SB_PAGES]
    bias_ref, gain_ref, o_ref, q_scr, acc_scr, run_scr = rest[2 * SB_PAGES:]
    j = pl.program_id(1)
    nr = N_HEADS * n_new
    w2 = _suffix_weights()
    row = lax.broadcasted_iota(jnp.int32, (nr, LANE), 0)
    col = lax.broadcasted_iota(jnp.int32, (nr, LANE), 1)
    own = (col & (N_HEADS - 1)) == (row >> (n_new.bit_length() - 1))

    def visit(k_rows, v_rows, bias):
        n = len(k_rows)
        q = q_scr[...]
        zb, hl, rt, a = [None] * n, [None] * n, [None] * n, [None] * n
        state = {"run": run_scr[...], "acc": acc_scr[...]}

        def scores(p):
            z = lax.dot_general(q, k_rows[p](), NT_DIMS, preferred_element_type=F32)
            zb[p] = [z[:, m * LANE:(m + 1) * LANE] + bias
                     for m in range(z.shape[1] // LANE - 1, -1, -1)]

        def softplus(p):
            hl[p] = _sb_softplus(zb[p])

        def suffix(p):
            rt[p] = jnp.dot(hl[p], w2, preferred_element_type=F32)

        def weights(p):
            blocks, state["run"] = _sb_weights(zb[p], rt[p], state["run"])
            blocks = blocks[::-1]
            a[p] = (blocks[0] if len(blocks) == 1
                    else jnp.concatenate(blocks, axis=1)).astype(BF16)

        def accumulate(p):
            state["acc"] = state["acc"] + jnp.dot(a[p], v_rows[p](), preferred_element_type=F32)

        _skewed(n, [scores, softplus, suffix, weights, accumulate])
        acc_scr[...] = state["acc"]
        run_scr[...] = state["run"]

    @pl.when(j == 0)
    def _():
        q = q_ref[...] * (ATT_SCALE * LOG2E)
        q_scr[...] = jnp.concatenate(
            [q[:, h * HEAD_DIM:(h + 1) * HEAD_DIM] for h in range(N_HEADS)], axis=0).astype(BF16)
        run_scr[...] = jnp.zeros_like(run_scr)
        acc_scr[...] = jnp.zeros_like(acc_scr)
        pad = jnp.zeros((LANE - nr, HEAD_DIM), F32)
        kb = jnp.concatenate([kn_ref[...], pad], axis=0).astype(BF16)
        vb = jnp.concatenate([vn_ref[...], pad], axis=0).astype(BF16)
        earlier = (col >> (N_HEADS.bit_length() - 1)) < (row & (n_new - 1))
        visit([lambda: kb], [lambda: vb], jnp.where(own & earlier, bias_ref[...], MASKED))

    def page_rows(ref):
        return lambda: ref[...].reshape(PAGE * N_HEADS, HEAD_DIM).astype(BF16)
    visit([page_rows(r) for r in kc_refs], [page_rows(r) for r in vc_refs],
          jnp.where(own, bias_ref[...], MASKED))

    @pl.when(j == pl.num_programs(1) - 1)
    def _():
        acc = acc_scr[...]
        heads = [acc[h * n_new:(h + 1) * n_new] for h in range(N_HEADS)]
        heads = [o * lax.rsqrt(jnp.mean(o * o, axis=-1, keepdims=True) + EPS) for o in heads]
        o_ref[...] = jnp.concatenate(heads, axis=1) * gain_ref[...]


def _sb_sample(pq, q_col0, k_new, v_new, cache_k, cache_v, page_table, bias_rep, gain,
               n_seq, n_new):
    n_pages = page_table.shape[1]
    qoff = q_col0 // D_GROUP
    nr = N_HEADS * n_new
    assert nr <= LANE
    new = pl.BlockSpec((None, nr, HEAD_DIM), lambda b, j, pt: (b, 0, 0))

    def page(r):
        return pl.BlockSpec(
            (None, None, PAGE, N_HEADS, HEAD_DIM),
            lambda b, j, pt: (0, pt[b, n_pages - 1 - (j * SB_PAGES + r)], 0, 0, 0))
    pages = [page(r) for r in range(SB_PAGES)]
    grid_spec = pltpu.PrefetchScalarGridSpec(
        num_scalar_prefetch=1,
        grid=(n_seq, n_pages // SB_PAGES),
        in_specs=[pl.BlockSpec((n_new, D_GROUP), lambda b, j, pt: (b, qoff)), new, new,
                  *pages, *pages,
                  pl.BlockSpec((nr, LANE), lambda b, j, pt: (0, 0)),
                  pl.BlockSpec((1, D_GROUP), lambda b, j, pt: (0, 0))],
        out_specs=pl.BlockSpec((n_new, D_GROUP), lambda b, j, pt: (b, 0)),
        scratch_shapes=[pltpu.VMEM((nr, HEAD_DIM), BF16), pltpu.VMEM((nr, HEAD_DIM), F32),
                        pltpu.VMEM((nr, LANE), F32)],
    )
    return pl.pallas_call(
        functools.partial(_sbs_kernel, n_new=n_new),
        grid_spec=grid_spec,
        out_shape=jax.ShapeDtypeStruct((n_seq * n_new, D_GROUP), F32),
        compiler_params=_params(("arbitrary", "arbitrary")),
        name="stickbreak_sample",
    )(page_table, pq, k_new, v_new, *([cache_k] * SB_PAGES), *([cache_v] * SB_PAGES),
      bias_rep, gain.reshape(1, D_GROUP))


def kernel(x_prompt, x_sample, cache_k, cache_v, state_hgrn, page_table, c_prompt, c_sample,
           lb_logits, norm_ffn1, norm_mix, norm_ffn2, w_mod, b_mod,
           w_ffn1_gate, w_ffn1_up, w_ffn1_down, w_in, g_out_a, g_out_b, b_sb, w_out,
           w_ffn2_gate, w_ffn2_up, w_ffn2_down, norm_final, w_final_mod, b_final_mod):
    n_p, seq, d = x_prompt.shape
    n_s, n_new, _ = x_sample.shape
    assert w_mod.shape[0] == 1, "single-layer trunk"
    assert n_p + n_s <= MOD_ROWS
    assert n_new & (n_new - 1) == 0 and n_new <= PAGE
    assert page_table.shape[1] % SB_PAGES == 0 and seq % SB_TILE == 0

    c_rows = jnp.concatenate(
        [c_prompt, c_sample, jnp.zeros((MOD_ROWS - n_p - n_s, d), F32)], axis=0)
    mod = _modulation(c_rows, w_mod[0], b_mod[0], PROJ_COLS).reshape(MOD_ROWS, N_MOD, d)
    fmod = _modulation(c_rows, w_final_mod, b_final_mod, PROJ_COLS).reshape(MOD_ROWS, 2, d)

    w_out_b = w_out[0].astype(BF16)
    bias2 = b_sb[0] * LOG2E
    bias_rows = jnp.broadcast_to(bias2.reshape(N_HEADS, 1, 1), (N_HEADS, 1, SB_TILE))
    bias_rep = jnp.broadcast_to(jnp.repeat(bias2, n_new)[:, None], (N_HEADS * n_new, LANE))

    def layer(x, mods, fmods, mod_spec, mod_spec1, tm, tf, tm_proj, mixer, w_ffn1, w_ffn2, emit):
        x1, h2, *w1 = _ffn(x, norm_ffn1[0], mods[0:3], *w_ffn1, norm_mix[0], mods[3:5],
                           mod_spec, tm, tf, final=False, emit=emit)
        pa = _in_proj(h2, w_in[0], 0, 5 * D_GROUP, tm_proj, PROJ_COLS)
        k_new = _in_proj(h2, w_in[0], 5 * D_GROUP, D_GROUP, tm_proj, PROJ_COLS)
        v_new = _in_proj(h2, w_in[0], 6 * D_GROUP, D_GROUP, tm_proj, PROJ_COLS)
        oa, ob, s_fin = mixer(pa, k_new, v_new)
        x2 = _out_proj(oa, ob, w_out_b, x1, mods[5], mod_spec1, tm)
        y, *w2 = _ffn(x2, norm_ffn2[0], mods[6:9], *w_ffn2, norm_final, fmods,
                      mod_spec, tm, tf, final=True, emit=emit)
        return y, k_new, v_new, s_fin, w1, w2

    tiles_per_seq = seq // FFN_ROWS
    mods_p = [mod[:n_p, j].reshape(n_p, 1, d) for j in range(N_MOD)]
    fmods_p = [fmod[:n_p, j].reshape(n_p, 1, d) for j in range(2)]
    spec_p = pl.BlockSpec((None, 1, d), lambda i, f: (i // tiles_per_seq, 0, 0))
    spec_p1 = pl.BlockSpec((None, 1, d), lambda i: (i // tiles_per_seq, 0, 0))

    def mixer_p(pa, k_new, v_new):
        s0 = jnp.zeros((n_p, N_HEADS, HEAD_DIM, HEAD_DIM), F32)
        oa, s_fin = _gla(pa, lb_logits[0], g_out_a[0], s0, n_p, seq, GLA_CHUNK, LANE, BF16)
        ob = _sb_prompt(pa, 4 * D_GROUP, k_new, v_new, bias_rows, g_out_b[0], n_p, seq)
        return oa, ob, s_fin

    rows_s = n_s * n_new
    mods_s = [jnp.repeat(mod[n_p:n_p + n_s, j], n_new, axis=0).reshape(1, rows_s, d)
              for j in range(N_MOD)]
    fmods_s = [jnp.repeat(fmod[n_p:n_p + n_s, j], n_new, axis=0).reshape(1, rows_s, d)
               for j in range(2)]
    spec_s = pl.BlockSpec((None, rows_s, d), lambda i, f: (0, 0, 0))
    spec_s1 = pl.BlockSpec((None, rows_s, d), lambda i: (0, 0, 0))

    def mixer_s(pa, k_new, v_new):
        oa, s_fin = _gla(pa, lb_logits[0], g_out_a[0], state_hgrn[0], n_s, LANE, LANE, n_new, F32)
        rows_th = (n_s, n_new * N_HEADS, HEAD_DIM)
        ob = _sb_sample(pa, 4 * D_GROUP, k_new.reshape(rows_th), v_new.reshape(rows_th),
                        cache_k, cache_v, page_table, bias_rep, g_out_b[0], n_s, n_new)
        return oa, ob, s_fin

    y_s, k_s, v_s, s_s, w1, w2 = layer(
        x_sample.reshape(rows_s, d), mods_s, fmods_s, spec_s, spec_s1, rows_s, FFN_COLS, rows_s,
        mixer_s, (w_ffn1_gate[0], w_ffn1_up[0], w_ffn1_down[0]),
        (w_ffn2_gate[0], w_ffn2_up[0], w_ffn2_down[0]), True)
    y_p, k_p, v_p, s_p, _, _ = layer(
        x_prompt.reshape(n_p * seq, d), mods_p, fmods_p, spec_p, spec_p1, FFN_ROWS, FFN_COLS,
        PROJ_ROWS, mixer_p, w1, w2, False)

    return (y_p.reshape(n_p, seq, d), y_s.reshape(n_s, n_new, d),
            k_p.reshape(1, n_p, seq, N_HEADS, HEAD_DIM), v_p.reshape(1, n_p, seq, N_HEADS, HEAD_DIM),
            k_s.reshape(1, n_s, n_new, N_HEADS, HEAD_DIM), v_s.reshape(1, n_s, n_new, N_HEADS, HEAD_DIM),
            s_p[None], s_s[None])
```

```python
import functools

import jax
import jax.numpy as jnp
from jax import lax
from jax.experimental import pallas as pl
from jax.experimental.pallas import tpu as pltpu

F32 = jnp.float32
BF16 = jnp.bfloat16

N_HEADS = 8
HEAD_DIM = 128
D_GROUP = N_HEADS * HEAD_DIM
N_MOD = 9
GLA_CHUNK = 32
GLA_GROUPS = 8
PAGE = 128
EPS = 1e-6
ATT_SCALE = HEAD_DIM ** -0.5
LOG2E = 1.4426950408889634
MASKED = -1e30
MOD_ROWS = 16
LANE = 128
SB_TILE = 256
SB_PAGES = 16
FFN_ROWS = 512
FFN2_ROWS = 1024
FFN_COLS = 512
EPILOGUE_ROWS = 256
DOWN_COLS = 512
PROJ_ROWS = 1024
PROJ_COLS = 1024
VMEM_LIMIT = 63 * 1024 * 1024

NT_DIMS = (((1,), (1,)), ((), ()))


def _params(sem):
    return pltpu.CompilerParams(dimension_semantics=sem, vmem_limit_bytes=VMEM_LIMIT)


def _sigmoid(x):
    return 1.0 / (1.0 + jnp.exp(-x))


def _silu(x):
    return x * _sigmoid(x)


def _adaln(x, nw, shift, scale):
    ms = jnp.mean(x * x, axis=-1, keepdims=True)
    return x * lax.rsqrt(ms + EPS) * (nw * (1.0 + scale)) + shift


def _head_norm(o, gain):
    ms = jnp.mean(o * o, axis=-1, keepdims=True)
    return o * lax.rsqrt(ms + EPS) * gain


def _split2(x):
    hi = x.astype(BF16)
    lo = (x - hi.astype(F32)).astype(BF16)
    return hi, lo


def _mod_kernel(c_ref, w_ref, b_ref, o_ref):
    a = _silu(c_ref[...]).astype(BF16)
    o_ref[...] = jnp.dot(a, w_ref[...].astype(BF16), preferred_element_type=F32) + b_ref[...]


def _modulation(c_rows, w, b, tn):
    d, n = w.shape
    return pl.pallas_call(
        _mod_kernel,
        grid=(n // tn,),
        in_specs=[
            pl.BlockSpec((MOD_ROWS, d), lambda j: (0, 0)),
            pl.BlockSpec((d, tn), lambda j: (0, j)),
            pl.BlockSpec((1, tn), lambda j: (0, j)),
        ],
        out_specs=pl.BlockSpec((MOD_ROWS, tn), lambda j: (0, j)),
        out_shape=jax.ShapeDtypeStruct((MOD_ROWS, n), F32),
        compiler_params=_params(("arbitrary",)),
        name="modulation",
    )(c_rows, w, b.reshape(1, n))


def _ffn_kernel(x_ref, nw_ref, sh_ref, sc_ref, ga_ref, wg_ref, wu_ref, wd_ref,
                nw2_ref, sh2_ref, sc2_ref, *rest, final, emit):
    n_out = (1 if final else 2) + (3 if emit else 0)
    outs, scratch = rest[:n_out], rest[n_out:]
    y_ref = outs[0] if final else outs[1]
    h_scr = scratch[0]
    acc_scr = y_ref if final else scratch[1]
    f = pl.program_id(1)

    @pl.when(f == 0)
    def _():
        h = _adaln(x_ref[...], nw_ref[...], sh_ref[...], sc_ref[...])
        h_scr[...] = h.astype(BF16)
        acc_scr[...] = jnp.zeros_like(acc_scr)

    wg, wu, wd = wg_ref[...], wu_ref[...], wd_ref[...]
    if emit:
        wg, wu, wd = wg.astype(BF16), wu.astype(BF16), wd.astype(BF16)
        outs[-3][...], outs[-2][...], outs[-1][...] = wg, wu, wd
    h = h_scr[...]
    g = jnp.dot(h, wg, preferred_element_type=F32)
    u = jnp.dot(h, wu, preferred_element_type=F32)
    a = (_silu(g) * u).astype(BF16)
    for c in range(0, acc_scr.shape[1], DOWN_COLS):
        cols = slice(c, c + DOWN_COLS)
        acc_scr[:, cols] += jnp.dot(a, wd[:, cols], preferred_element_type=F32)

    @pl.when(f == pl.num_programs(1) - 1)
    def _():
        tm = x_ref.shape[0]
        step = EPILOGUE_ROWS if tm % EPILOGUE_ROWS == 0 else tm
        for r in range(0, tm, step):
            rows = slice(r, r + step)
            mod = [m[...] if m.shape[0] == 1 else m[rows, :] for m in (ga_ref, sh2_ref, sc2_ref)]
            xn = x_ref[rows, :] + 0.5 * mod[0] * acc_scr[rows, :]
            y = _adaln(xn, nw2_ref[...], mod[1], mod[2])
            if not final:
                outs[0][rows, :] = xn
            y_ref[rows, :] = y.astype(y_ref.dtype)


def _ffn(x, nw, mods, wg, wu, wd, nw2, mods2, mod_spec, tm, tf, final, emit):
    n, d = x.shape
    nf = wg.shape[1]
    assert not emit or n == tm
    row = pl.BlockSpec((tm, d), lambda i, f: (i, 0))
    vec = pl.BlockSpec((1, d), lambda i, f: (0, 0))
    w_specs = [pl.BlockSpec((d, tf), lambda i, f: (0, f)),
               pl.BlockSpec((d, tf), lambda i, f: (0, f)),
               pl.BlockSpec((tf, d), lambda i, f: (f, 0))]
    in_specs = [row, vec, mod_spec, mod_spec, mod_spec, *w_specs, vec, mod_spec, mod_spec]
    out_specs = [row] if final else [row, row]
    out_shape = ([jax.ShapeDtypeStruct((n, d), F32)] if final else
                 [jax.ShapeDtypeStruct((n, d), F32), jax.ShapeDtypeStruct((n, d), BF16)])
    if emit:
        out_specs += w_specs
        out_shape += [jax.ShapeDtypeStruct(w.shape, BF16) for w in (wg, wu, wd)]
    return pl.pallas_call(
        functools.partial(_ffn_kernel, final=final, emit=emit),
        grid=(n // tm, nf // tf),
        in_specs=in_specs,
        out_specs=out_specs,
        out_shape=out_shape,
        scratch_shapes=[pltpu.VMEM((tm, d), BF16)] + ([] if final else [pltpu.VMEM((tm, d), F32)]),
        compiler_params=_params(("arbitrary", "arbitrary")),
        name="ffn_final" if final else "ffn",
    )(x, nw.reshape(1, d), mods[0], mods[1], mods[2], wg, wu, wd,
      nw2.reshape(1, d), mods2[0], mods2[1])


def _in_proj_kernel(h_ref, w_ref, o_ref, wb_scr):
    @pl.when(pl.program_id(1) == 0)
    def _():
        wb_scr[...] = w_ref[...].astype(BF16)

    o_ref[...] = jnp.dot(h_ref[...], wb_scr[...], preferred_element_type=F32)


def _in_proj(h, w, col0, ncols, tm, tn):
    n, d = h.shape
    off = col0 // tn
    return pl.pallas_call(
        _in_proj_kernel,
        grid=(ncols // tn, n // tm),
        in_specs=[
            pl.BlockSpec((tm, d), lambda j, i: (i, 0)),
            pl.BlockSpec((d, tn), lambda j, i: (0, j + off)),
        ],
        out_specs=pl.BlockSpec((tm, tn), lambda j, i: (i, j)),
        out_shape=jax.ShapeDtypeStruct((n, ncols), F32),
        scratch_shapes=[pltpu.VMEM((d, tn), BF16)],
        compiler_params=_params(("arbitrary", "arbitrary")),
        name="in_proj",
    )(h, w)


def _out_proj_kernel(oa_ref, ob_ref, w_ref, x_ref, ga_ref, o_ref):
    m = jnp.dot(oa_ref[...].astype(BF16), w_ref[:D_GROUP, :], preferred_element_type=F32)
    m += jnp.dot(ob_ref[...].astype(BF16), w_ref[D_GROUP:, :], preferred_element_type=F32)
    o_ref[...] = x_ref[...] + ga_ref[...] * m


def _out_proj(oa, ob, w, x, gate, mod_spec2, tm):
    n, d = x.shape
    half = pl.BlockSpec((tm, D_GROUP), lambda i: (i, 0))
    row = pl.BlockSpec((tm, d), lambda i: (i, 0))
    return pl.pallas_call(
        _out_proj_kernel,
        grid=(n // tm,),
        in_specs=[half, half, pl.BlockSpec((2 * D_GROUP, d), lambda i: (0, 0)), row, mod_spec2],
        out_specs=row,
        out_shape=jax.ShapeDtypeStruct((n, d), F32),
        compiler_params=_params(("arbitrary",)),
        name="out_proj",
    )(oa, ob, w, x, gate)


def _gla_kernel(q_ref, f_ref, i_ref, g_ref, lbl_ref, gain_ref, s0_ref, o_ref, s_ref, st_scr,
                *, chunk, n_valid, n_groups):
    t = pl.program_id(1)
    n_chunks = LANE // chunk
    shift = chunk.bit_length() - 1

    @pl.when(t == 0)
    def _():
        for h in range(N_HEADS):
            st_scr[h] = s0_ref[h].T

    r_id = lax.broadcasted_iota(jnp.int32, (LANE, LANE), 0)
    c_id = lax.broadcasted_iota(jnp.int32, (LANE, LANE), 1)
    same = (r_id >> shift) == (c_id >> shift)
    causal = same & (c_id <= r_id)
    m_cum = jnp.concatenate([jnp.where(causal, 1.0, 0.0), jnp.where(same, 1.0, 0.0)],
                            axis=0).astype(BF16)

    heads = [slice(h * HEAD_DIM, (h + 1) * HEAD_DIM) for h in range(N_HEADS)]
    lbl = lbl_ref[...]
    mx = jnp.maximum(lbl, 0.0)
    e1 = jnp.exp(lbl - mx)
    lb = e1 / (e1 + jnp.exp(-mx))

    rows = [slice(g * LANE, (g + 1) * LANE) for g in range(n_groups)]
    kk, bb, v, vb, qd, k_inv, k_end, decay, o, v_t = ([None] * n_groups for _ in range(10))
    st = [st_scr[h] for h in range(N_HEADS)]

    def load(ref, g):
        if n_valid < LANE:
            return jnp.concatenate([ref[...], jnp.zeros((LANE - n_valid, D_GROUP), F32)], axis=0)
        return ref[rows[g], :]

    def gates(g):
        f = lb + (1.0 - lb) * _sigmoid(load(f_ref, g))
        logf = jnp.log(f)
        k = 1.0 - f
        if n_valid < LANE:
            valid = lax.broadcasted_iota(jnp.int32, (LANE, D_GROUP), 0) < n_valid
            logf = jnp.where(valid, logf, 0.0)
            k = jnp.where(valid, k, 0.0)
        hi = logf.astype(BF16)
        r1 = logf - hi.astype(F32)
        mid = r1.astype(BF16)
        lo = (r1 - mid.astype(F32)).astype(BF16)
        kk[g] = k
        bb[g] = jnp.dot(m_cum, jnp.concatenate([hi, mid, lo], axis=1),
                        preferred_element_type=F32)

    def decays(g):
        s = bb[g][:, :D_GROUP] + bb[g][:, D_GROUP:2 * D_GROUP] + bb[g][:, 2 * D_GROUP:]
        b = s[:LANE]
        b_last = s[LANE:]
        v[g] = load(i_ref, g)
        vb[g] = v[g].astype(BF16)
        qd[g] = (load(q_ref, g) * ATT_SCALE * jnp.exp(b)).astype(BF16)
        k_inv[g] = (kk[g] * jnp.exp(-b)).astype(BF16)
        k_end[g] = (kk[g] * jnp.exp(b_last - b)).astype(BF16)
        decay[g] = jnp.exp(b_last)

    def intra(g):
        a = [lax.dot_general(qd[g][:, c], k_inv[g][:, c], NT_DIMS, preferred_element_type=F32)
             for c in heads]
        a = [jnp.where(causal, x, 0.0).astype(BF16) for x in a]
        o[g] = [jnp.dot(x, vb[g][:, c], preferred_element_type=F32) for x, c in zip(a, heads)]
        v_t[g] = [[v[g][j * chunk:(j + 1) * chunk, c].T.astype(BF16) for j in range(n_chunks)]
                  for c in heads]

    def state(g):
        o_state = [[] for _ in heads]
        for j in range(n_chunks):
            for h, c in enumerate(heads):
                o_state[h].append(lax.dot_general(qd[g][j * chunk:(j + 1) * chunk, c],
                                                  st[h].astype(BF16), NT_DIMS,
                                                  preferred_element_type=F32))
                upd = jnp.dot(v_t[g][h][j], k_end[g][j * chunk:(j + 1) * chunk, c],
                              preferred_element_type=F32)
                st[h] = decay[g][j * chunk:j * chunk + 1, c] * st[h] + upd
        for h in range(N_HEADS):
            o[g][h] = o[g][h] + (jnp.concatenate(o_state[h], axis=0) if n_chunks > 1
                                 else o_state[h][0])

    def emit(g):
        gate = _silu(load(g_ref, g))
        for h, c in enumerate(heads):
            out = (_head_norm(o[g][h], gain_ref[:, c]) * gate[:, c]).astype(o_ref.dtype)
            if n_valid < LANE:
                o_ref[:, c] = out[:n_valid]
            else:
                o_ref[rows[g], c] = out

    _skewed(n_groups, [gates, decays, intra, state, emit])
    for h in range(N_HEADS):
        st_scr[h] = st[h]

    @pl.when(t == pl.num_programs(1) - 1)
    def _():
        for h in range(N_HEADS):
            s_ref[h] = st_scr[h].T


def _gla(pa, lb_logits, gain, s0, n_seq, rows, chunk, n_valid, out_dtype):
    n_groups = min(GLA_GROUPS, rows // LANE)
    n_tiles = rows // (n_groups * LANE)
    tile = n_groups * LANE if n_valid == LANE else n_valid
    assert n_valid == LANE or (rows == LANE and n_valid % 8 == 0)

    def col(group):
        return pl.BlockSpec((tile, D_GROUP), lambda b, t: (b * n_tiles + t, group))
    vec = pl.BlockSpec((1, D_GROUP), lambda b, t: (0, 0))
    state = pl.BlockSpec((None, N_HEADS, HEAD_DIM, HEAD_DIM), lambda b, t: (b, 0, 0, 0))
    return pl.pallas_call(
        functools.partial(_gla_kernel, chunk=chunk, n_valid=n_valid, n_groups=n_groups),
        grid=(n_seq, n_tiles),
        in_specs=[col(0), col(1), col(2), col(3), vec, vec, state],
        out_specs=(pl.BlockSpec((tile, D_GROUP), lambda b, t: (b * n_tiles + t, 0)), state),
        out_shape=(jax.ShapeDtypeStruct((n_seq * n_tiles * tile, D_GROUP), out_dtype),
                   jax.ShapeDtypeStruct((n_seq, N_HEADS, HEAD_DIM, HEAD_DIM), F32)),
        scratch_shapes=[pltpu.VMEM((N_HEADS, HEAD_DIM, HEAD_DIM), F32)],
        compiler_params=_params(("arbitrary", "arbitrary")),
        name="hgrn2",
    )(pa, pa, pa, pa, lb_logits.reshape(1, D_GROUP), gain.reshape(1, D_GROUP), s0)


def _suffix_weights():
    r_id = lax.broadcasted_iota(jnp.int32, (2 * LANE, 2 * LANE), 0) & (LANE - 1)
    c_id = lax.broadcasted_iota(jnp.int32, (2 * LANE, 2 * LANE), 1)
    return jnp.where((r_id >= c_id) | (c_id >= LANE), 1.0, 0.0).astype(BF16)


def _softplus2(z):
    return jnp.maximum(z, 0.0) + jnp.log(1.0 + jnp.exp2(-jnp.abs(z))) * LOG2E


def _sb_softplus(z_blocks):
    sp = [_softplus2(z) for z in z_blocks]
    hi, lo = _split2(sp[0] if len(sp) == 1 else jnp.concatenate(sp, axis=0))
    return jnp.concatenate([hi, lo], axis=1)


def _sb_weights(z_blocks, rt, run):
    m = z_blocks[0].shape[0]
    out = []
    for n, z in enumerate(z_blocks):
        within = rt[n * m:(n + 1) * m, :LANE]
        total = rt[n * m:(n + 1) * m, LANE:]
        out.append(jnp.exp2(z - within if run is None else z - within - run))
        run = total if run is None else run + total
    return out, run


def _skewed(n_items, stages):
    for tick in range(n_items + len(stages) - 1):
        for s in range(len(stages) - 1, -1, -1):
            i = tick - s
            if 0 <= i < n_items:
                stages[s](i)


def _sbp_kernel(q_ref, k_ref, v_ref, bias_ref, gain_ref, o_ref, q_scr, acc_scr, run_scr):
    i = pl.program_id(1)
    w2 = _suffix_weights()
    q_scr[...] = (q_ref[...] * (ATT_SCALE * LOG2E)).astype(BF16)
    acc_scr[...] = jnp.zeros_like(acc_scr)
    run_scr[...] = jnp.zeros_like(run_scr)
    r_id = lax.broadcasted_iota(jnp.int32, (SB_TILE, LANE), 0)
    c_id = lax.broadcasted_iota(jnp.int32, (SB_TILE, LANE), 1)
    n_blk = SB_TILE // LANE

    def visit(slabs, diagonal):
        n = len(slabs) * N_HEADS
        rows = [pl.ds(pl.multiple_of(j * SB_TILE, SB_TILE), SB_TILE) for j in slabs]
        order = range(n_blk - 1, -1, -1)
        if diagonal:
            mask = [jnp.where(c_id + m * LANE < r_id, 0.0, MASKED) for m in order]
        heads = [slice(h * HEAD_DIM, (h + 1) * HEAD_DIM) for h in range(N_HEADS)]
        zb, hl, rt = [None] * n, [None] * n, [None] * n

        def scores(t):
            r, c = rows[t // N_HEADS], heads[t % N_HEADS]
            z = lax.dot_general(q_scr[:, c], k_ref[r, c].astype(BF16), NT_DIMS,
                                preferred_element_type=F32) + bias_ref[t % N_HEADS]
            zb[t] = [z[:, m * LANE:(m + 1) * LANE] for m in order]
            if diagonal and t < N_HEADS:
                zb[t] = [z_m + m_m for z_m, m_m in zip(zb[t], mask)]

        def softplus(t):
            hl[t] = _sb_softplus(zb[t])

        def suffix(t):
            rt[t] = jnp.dot(hl[t], w2, preferred_element_type=F32)

        def accumulate(t):
            r, h = rows[t // N_HEADS], t % N_HEADS
            a_blocks, total = _sb_weights(zb[t], rt[t], None)
            a = jnp.concatenate(a_blocks[::-1], axis=1).astype(BF16)
            p = jnp.dot(a, v_ref[r, heads[h]].astype(BF16), preferred_element_type=F32)
            run = run_scr[h]
            acc_scr[h] += p * jnp.exp2(-run)
            run_scr[h] = run + total

        _skewed(n, [scores, softplus, suffix, accumulate])

    @pl.when(i == 0)
    def _():
        visit([0], True)

    @pl.when(i > 0)
    def _():
        visit([i, i - 1], True)

        def pair(t, carry):
            visit([i - 2 - 2 * t, i - 3 - 2 * t], False)
            return carry
        lax.fori_loop(0, (i - 1) // 2, pair, 0)

        @pl.when(i % 2 == 0)
        def _():
            visit([0], False)

    for h in range(N_HEADS):
        c = slice(h * HEAD_DIM, (h + 1) * HEAD_DIM)
        o_ref[:, c] = _head_norm(acc_scr[h], gain_ref[:, c]).astype(o_ref.dtype)


def _sb_prompt(pq, q_col0, k, v, bias_rows, gain, n_seq, seq):
    nq = seq // SB_TILE
    qoff = q_col0 // D_GROUP
    kv = pl.BlockSpec((seq, D_GROUP), lambda b, i: (b, 0))
    return pl.pallas_call(
        _sbp_kernel,
        grid=(n_seq, nq),
        in_specs=[
            pl.BlockSpec((SB_TILE, D_GROUP), lambda b, i: (b * nq + i, qoff)),
            kv, kv,
            pl.BlockSpec((N_HEADS, 1, SB_TILE), lambda b, i: (0, 0, 0)),
            pl.BlockSpec((1, D_GROUP), lambda b, i: (0, 0)),
        ],
        out_specs=pl.BlockSpec((SB_TILE, D_GROUP), lambda b, i: (b * nq + i, 0)),
        out_shape=jax.ShapeDtypeStruct((n_seq * seq, D_GROUP), BF16),
        scratch_shapes=[pltpu.VMEM((SB_TILE, D_GROUP), BF16),
                        pltpu.VMEM((N_HEADS, SB_TILE, HEAD_DIM), F32),
                        pltpu.VMEM((N_HEADS, SB_TILE, LANE), F32)],
        compiler_params=_params(("arbitrary", "arbitrary")),
        name="stickbreak_prompt",
    )(pq, k, v, bias_rows, gain.reshape(1, D_GROUP))


def _sbs_kernel(pt_ref, q_ref, kn_ref, vn_ref, *rest, n_new):
    del pt_ref
    kc_refs = rest[:SB_PAGES]
    vc_refs = rest[SB_PAGES:2 * SB_PAGES]
    bias_ref, gain_ref, o_ref, q_scr, acc_scr, run_scr = rest[2 * SB_PAGES:]
    j = pl.program_id(1)
    nr = N_HEADS * n_new
    w2 = _suffix_weights()
    row = lax.broadcasted_iota(jnp.int32, (nr, LANE), 0)
    col = lax.broadcasted_iota(jnp.int32, (nr, LANE), 1)
    own = (col & (N_HEADS - 1)) == (row >> (n_new.bit_length() - 1))

    def visit(k_rows, v_rows, bias):
        n = len(k_rows)
        q = q_scr[...]
        zb, hl, rt, a = [None] * n, [None] * n, [None] * n, [None] * n
        state = {"run": run_scr[...], "acc": acc_scr[...]}

        def scores(p):
            z = lax.dot_general(q, k_rows[p](), NT_DIMS, preferred_element_type=F32)
            zb[p] = [z[:, m * LANE:(m + 1) * LANE] + bias
                     for m in range(z.shape[1] // LANE - 1, -1, -1)]

        def softplus(p):
            hl[p] = _sb_softplus(zb[p])

        def suffix(p):
            rt[p] = jnp.dot(hl[p], w2, preferred_element_type=F32)

        def weights(p):
            blocks, state["run"] = _sb_weights(zb[p], rt[p], state["run"])
            blocks = blocks[::-1]
            a[p] = (blocks[0] if len(blocks) == 1
                    else jnp.concatenate(blocks, axis=1)).astype(BF16)

        def accumulate(p):
            state["acc"] = state["acc"] + jnp.dot(a[p], v_rows[p](), preferred_element_type=F32)

        _skewed(n, [scores, softplus, suffix, weights, accumulate])
        acc_scr[...] = state["acc"]
        run_scr[...] = state["run"]

    @pl.when(j == 0)
    def _():
        q = q_ref[...] * (ATT_SCALE * LOG2E)
        q_scr[...] = jnp.concatenate(
            [q[:, h * HEAD_DIM:(h + 1) * HEAD_DIM] for h in range(N_HEADS)], axis=0).astype(BF16)
        run_scr[...] = jnp.zeros_like(run_scr)
        acc_scr[...] = jnp.zeros_like(acc_scr)
        pad = jnp.zeros((LANE - nr, HEAD_DIM), F32)
        kb = jnp.concatenate([kn_ref[...], pad], axis=0).astype(BF16)
        vb = jnp.concatenate([vn_ref[...], pad], axis=0).astype(BF16)
        earlier = (col >> (N_HEADS.bit_length() - 1)) < (row & (n_new - 1))
        visit([lambda: kb], [lambda: vb], jnp.where(own & earlier, bias_ref[...], MASKED))

    def page_rows(ref):
        return lambda: ref[...].reshape(PAGE * N_HEADS, HEAD_DIM).astype(BF16)
    visit([page_rows(r) for r in kc_refs], [page_rows(r) for r in vc_refs],
          jnp.where(own, bias_ref[...], MASKED))

    @pl.when(j == pl.num_programs(1) - 1)
    def _():
        acc = acc_scr[...]
        heads = [acc[h * n_new:(h + 1) * n_new] for h in range(N_HEADS)]
        heads = [o * lax.rsqrt(jnp.mean(o * o, axis=-1, keepdims=True) + EPS) for o in heads]
        o_ref[...] = jnp.concatenate(heads, axis=1) * gain_ref[...]


def _sb_sample(pq, q_col0, k_new, v_new, cache_k, cache_v, page_table, bias_rep, gain,
               n_seq, n_new):
    n_pages = page_table.shape[1]
    qoff = q_col0 // D_GROUP
    nr = N_HEADS * n_new
    assert nr <= LANE
    new = pl.BlockSpec((None, nr, HEAD_DIM), lambda b, j, pt: (b, 0, 0))

    def page(r):
        return pl.BlockSpec(
            (None, None, PAGE, N_HEADS, HEAD_DIM),
            lambda b, j, pt: (0, pt[b, n_pages - 1 - (j * SB_PAGES + r)], 0, 0, 0))
    pages = [page(r) for r in range(SB_PAGES)]
    grid_spec = pltpu.PrefetchScalarGridSpec(
        num_scalar_prefetch=1,
        grid=(n_seq, n_pages // SB_PAGES),
        in_specs=[pl.BlockSpec((n_new, D_GROUP), lambda b, j, pt: (b, qoff)), new, new,
                  *pages, *pages,
                  pl.BlockSpec((nr, LANE), lambda b, j, pt: (0, 0)),
                  pl.BlockSpec((1, D_GROUP), lambda b, j, pt: (0, 0))],
        out_specs=pl.BlockSpec((n_new, D_GROUP), lambda b, j, pt: (b, 0)),
        scratch_shapes=[pltpu.VMEM((nr, HEAD_DIM), BF16), pltpu.VMEM((nr, HEAD_DIM), F32),
                        pltpu.VMEM((nr, LANE), F32)],
    )
    return pl.pallas_call(
        functools.partial(_sbs_kernel, n_new=n_new),
        grid_spec=grid_spec,
        out_shape=jax.ShapeDtypeStruct((n_seq * n_new, D_GROUP), F32),
        compiler_params=_params(("arbitrary", "arbitrary")),
        name="stickbreak_sample",
    )(page_table, pq, k_new, v_new, *([cache_k] * SB_PAGES), *([cache_v] * SB_PAGES),
      bias_rep, gain.reshape(1, D_GROUP))


def kernel(x_prompt, x_sample, cache_k, cache_v, state_hgrn, page_table, c_prompt, c_sample,
           lb_logits, norm_ffn1, norm_mix, norm_ffn2, w_mod, b_mod,
           w_ffn1_gate, w_ffn1_up, w_ffn1_down, w_in, g_out_a, g_out_b, b_sb, w_out,
           w_ffn2_gate, w_ffn2_up, w_ffn2_down, norm_final, w_final_mod, b_final_mod):
    n_p, seq, d = x_prompt.shape
    n_s, n_new, _ = x_sample.shape
    assert w_mod.shape[0] == 1, "single-layer trunk"
    assert n_p + n_s <= MOD_ROWS
    assert n_new & (n_new - 1) == 0 and n_new <= PAGE
    assert page_table.shape[1] % SB_PAGES == 0 and seq % SB_TILE == 0

    c_rows = jnp.concatenate(
        [c_prompt, c_sample, jnp.zeros((MOD_ROWS - n_p - n_s, d), F32)], axis=0)
    mod = _modulation(c_rows, w_mod[0], b_mod[0], PROJ_COLS).reshape(MOD_ROWS, N_MOD, d)
    fmod = _modulation(c_rows, w_final_mod, b_final_mod, PROJ_COLS).reshape(MOD_ROWS, 2, d)

    w_out_b = w_out[0].astype(BF16)
    bias2 = b_sb[0] * LOG2E
    bias_rows = jnp.broadcast_to(bias2.reshape(N_HEADS, 1, 1), (N_HEADS, 1, SB_TILE))
    bias_rep = jnp.broadcast_to(jnp.repeat(bias2, n_new)[:, None], (N_HEADS * n_new, LANE))

    def layer(x, mods, fmods, mod_spec, mod_spec1, mod_spec2, tm, tm2, tf, tm_proj, mixer,
              w_ffn1, w_ffn2, emit):
        x1, h2, *w1 = _ffn(x, norm_ffn1[0], mods[0:3], *w_ffn1, norm_mix[0], mods[3:5],
                           mod_spec, tm, tf, final=False, emit=emit)
        pa = _in_proj(h2, w_in[0], 0, 5 * D_GROUP, tm_proj, PROJ_COLS)
        k_new = _in_proj(h2, w_in[0], 5 * D_GROUP, D_GROUP, tm_proj, PROJ_COLS)
        v_new = _in_proj(h2, w_in[0], 6 * D_GROUP, D_GROUP, tm_proj, PROJ_COLS)
        oa, ob, s_fin = mixer(pa, k_new, v_new)
        x2 = _out_proj(oa, ob, w_out_b, x1, mods[5], mod_spec1, tm)
        y, *w2 = _ffn(x2, norm_ffn2[0], mods[6:9], *w_ffn2, norm_final, fmods,
                      mod_spec2, tm2, tf, final=True, emit=emit)
        return y, k_new, v_new, s_fin, w1, w2

    tiles_per_seq = seq // FFN_ROWS
    mods_p = [mod[:n_p, j].reshape(n_p, 1, d) for j in range(N_MOD)]
    fmods_p = [fmod[:n_p, j].reshape(n_p, 1, d) for j in range(2)]
    spec_p = pl.BlockSpec((None, 1, d), lambda i, f: (i // tiles_per_seq, 0, 0))
    spec_p1 = pl.BlockSpec((None, 1, d), lambda i: (i // tiles_per_seq, 0, 0))
    spec_p2 = pl.BlockSpec((None, 1, d), lambda i, f: (i // (seq // FFN2_ROWS), 0, 0))

    def mixer_p(pa, k_new, v_new):
        s0 = jnp.zeros((n_p, N_HEADS, HEAD_DIM, HEAD_DIM), F32)
        oa, s_fin = _gla(pa, lb_logits[0], g_out_a[0], s0, n_p, seq, GLA_CHUNK, LANE, BF16)
        ob = _sb_prompt(pa, 4 * D_GROUP, k_new, v_new, bias_rows, g_out_b[0], n_p, seq)
        return oa, ob, s_fin

    rows_s = n_s * n_new
    mods_s = [jnp.repeat(mod[n_p:n_p + n_s, j], n_new, axis=0).reshape(1, rows_s, d)
              for j in range(N_MOD)]
    fmods_s = [jnp.repeat(fmod[n_p:n_p + n_s, j], n_new, axis=0).reshape(1, rows_s, d)
               for j in range(2)]
    spec_s = pl.BlockSpec((None, rows_s, d), lambda i, f: (0, 0, 0))
    spec_s1 = pl.BlockSpec((None, rows_s, d), lambda i: (0, 0, 0))

    def mixer_s(pa, k_new, v_new):
        oa, s_fin = _gla(pa, lb_logits[0], g_out_a[0], state_hgrn[0], n_s, LANE, LANE, n_new, F32)
        rows_th = (n_s, n_new * N_HEADS, HEAD_DIM)
        ob = _sb_sample(pa, 4 * D_GROUP, k_new.reshape(rows_th), v_new.reshape(rows_th),
                        cache_k, cache_v, page_table, bias_rep, g_out_b[0], n_s, n_new)
        return oa, ob, s_fin

    y_s, k_s, v_s, s_s, w1, w2 = layer(
        x_sample.reshape(rows_s, d), mods_s, fmods_s, spec_s, spec_s1, spec_s, rows_s, rows_s,
        FFN_COLS, rows_s, mixer_s, (w_ffn1_gate[0], w_ffn1_up[0], w_ffn1_down[0]),
        (w_ffn2_gate[0], w_ffn2_up[0], w_ffn2_down[0]), True)
    y_p, k_p, v_p, s_p, _, _ = layer(
        x_prompt.reshape(n_p * seq, d), mods_p, fmods_p, spec_p, spec_p1, spec_p2, FFN_ROWS,
        FFN2_ROWS, FFN_COLS, PROJ_ROWS, mixer_p, w1, w2, False)

    return (y_p.reshape(n_p, seq, d), y_s.reshape(n_s, n_new, d),
            k_p.reshape(1, n_p, seq, N_HEADS, HEAD_DIM), v_p.reshape(1, n_p, seq, N_HEADS, HEAD_DIM),
            k_s.reshape(1, n_s, n_new, N_HEADS, HEAD_DIM), v_s.reshape(1, n_s, n_new, N_HEADS, HEAD_DIM),
            s_p[None], s_s[None])
```

```python
import functools

import jax
import jax.numpy as jnp
from jax import lax
from jax.experimental import pallas as pl
from jax.experimental.pallas import tpu as pltpu

F32 = jnp.float32
BF16 = jnp.bfloat16

N_HEADS = 8
HEAD_DIM = 128
D_GROUP = N_HEADS * HEAD_DIM
N_MOD = 9
GLA_CHUNK = 32
GLA_GROUPS = 8
PAGE = 128
EPS = 1e-6
ATT_SCALE = HEAD_DIM ** -0.5
LOG2E = 1.4426950408889634
MASKED = -1e30
MOD_ROWS = 16
LANE = 128
SB_TILE = 256
SB_PAGES = 16
FFN_ROWS = 512
FFN2_ROWS = 1024
FFN_COLS = 512
EPILOGUE_ROWS = 256
DOWN_COLS = 512
PROJ_ROWS = 1024
PROJ_COLS = 1024
VMEM_LIMIT = 63 * 1024 * 1024

NT_DIMS = (((1,), (1,)), ((), ()))


def _params(sem):
    return pltpu.CompilerParams(dimension_semantics=sem, vmem_limit_bytes=VMEM_LIMIT)


def _sigmoid(x):
    return 1.0 / (1.0 + jnp.exp(-x))


def _silu(x):
    return x * _sigmoid(x)


def _adaln(x, nw, shift, scale):
    ms = jnp.mean(x * x, axis=-1, keepdims=True)
    return x * lax.rsqrt(ms + EPS) * (nw * (1.0 + scale)) + shift


def _head_norm(o, gain):
    ms = jnp.mean(o * o, axis=-1, keepdims=True)
    return o * lax.rsqrt(ms + EPS) * gain


def _split2(x):
    hi = x.astype(BF16)
    lo = (x - hi.astype(F32)).astype(BF16)
    return hi, lo


def _mod_kernel(c_ref, w_ref, b_ref, o_ref):
    a = _silu(c_ref[...]).astype(BF16)
    o_ref[...] = jnp.dot(a, w_ref[...].astype(BF16), preferred_element_type=F32) + b_ref[...]


def _modulation(c_rows, w, b, tn):
    d, n = w.shape
    return pl.pallas_call(
        _mod_kernel,
        grid=(n // tn,),
        in_specs=[
            pl.BlockSpec((MOD_ROWS, d), lambda j: (0, 0)),
            pl.BlockSpec((d, tn), lambda j: (0, j)),
            pl.BlockSpec((1, tn), lambda j: (0, j)),
        ],
        out_specs=pl.BlockSpec((MOD_ROWS, tn), lambda j: (0, j)),
        out_shape=jax.ShapeDtypeStruct((MOD_ROWS, n), F32),
        compiler_params=_params(("arbitrary",)),
        name="modulation",
    )(c_rows, w, b.reshape(1, n))


def _ffn_kernel(x_ref, nw_ref, sh_ref, sc_ref, ga_ref, wg_ref, wu_ref, wd_ref,
                nw2_ref, sh2_ref, sc2_ref, *rest, final, emit):
    n_out = (1 if final else 2) + (3 if emit else 0)
    outs, scratch = rest[:n_out], rest[n_out:]
    y_ref = outs[0] if final else outs[1]
    h_scr = scratch[0]
    acc_scr = y_ref if final else scratch[1]
    f = pl.program_id(1)

    @pl.when(f == 0)
    def _():
        h = _adaln(x_ref[...], nw_ref[...], sh_ref[...], sc_ref[...])
        h_scr[...] = h.astype(BF16)
        acc_scr[...] = jnp.zeros_like(acc_scr)

    wg, wu, wd = wg_ref[...], wu_ref[...], wd_ref[...]
    if emit:
        wg, wu, wd = wg.astype(BF16), wu.astype(BF16), wd.astype(BF16)
        outs[-3][...], outs[-2][...], outs[-1][...] = wg, wu, wd
    h = h_scr[...]
    g = jnp.dot(h, wg, preferred_element_type=F32)
    u = jnp.dot(h, wu, preferred_element_type=F32)
    a = (_silu(g) * u).astype(BF16)
    for c in range(0, acc_scr.shape[1], DOWN_COLS):
        cols = slice(c, c + DOWN_COLS)
        acc_scr[:, cols] += jnp.dot(a, wd[:, cols], preferred_element_type=F32)

    @pl.when(f == pl.num_programs(1) - 1)
    def _():
        tm = x_ref.shape[0]
        step = EPILOGUE_ROWS if tm % EPILOGUE_ROWS == 0 else tm
        for r in range(0, tm, step):
            rows = slice(r, r + step)
            mod = [m[...] if m.shape[0] == 1 else m[rows, :] for m in (ga_ref, sh2_ref, sc2_ref)]
            xn = x_ref[rows, :] + 0.5 * mod[0] * acc_scr[rows, :]
            y = _adaln(xn, nw2_ref[...], mod[1], mod[2])
            if not final:
                outs[0][rows, :] = xn
            y_ref[rows, :] = y.astype(y_ref.dtype)


def _ffn(x, nw, mods, wg, wu, wd, nw2, mods2, mod_spec, tm, tf, final, emit):
    n, d = x.shape
    nf = wg.shape[1]
    assert not emit or n == tm
    row = pl.BlockSpec((tm, d), lambda i, f: (i, 0))
    vec = pl.BlockSpec((1, d), lambda i, f: (0, 0))
    w_specs = [pl.BlockSpec((d, tf), lambda i, f: (0, f)),
               pl.BlockSpec((d, tf), lambda i, f: (0, f)),
               pl.BlockSpec((tf, d), lambda i, f: (f, 0))]
    in_specs = [row, vec, mod_spec, mod_spec, mod_spec, *w_specs, vec, mod_spec, mod_spec]
    out_specs = [row] if final else [row, row]
    out_shape = ([jax.ShapeDtypeStruct((n, d), F32)] if final else
                 [jax.ShapeDtypeStruct((n, d), F32), jax.ShapeDtypeStruct((n, d), BF16)])
    if emit:
        out_specs += w_specs
        out_shape += [jax.ShapeDtypeStruct(w.shape, BF16) for w in (wg, wu, wd)]
    return pl.pallas_call(
        functools.partial(_ffn_kernel, final=final, emit=emit),
        grid=(n // tm, nf // tf),
        in_specs=in_specs,
        out_specs=out_specs,
        out_shape=out_shape,
        scratch_shapes=[pltpu.VMEM((tm, d), BF16)] + ([] if final else [pltpu.VMEM((tm, d), F32)]),
        compiler_params=_params(("arbitrary", "arbitrary")),
        name="ffn_final" if final else "ffn",
    )(x, nw.reshape(1, d), mods[0], mods[1], mods[2], wg, wu, wd,
      nw2.reshape(1, d), mods2[0], mods2[1])


def _in_proj_kernel(h_ref, w_ref, o_ref, wb_scr):
    @pl.when(pl.program_id(1) == 0)
    def _():
        wb_scr[...] = w_ref[...].astype(BF16)

    o_ref[...] = jnp.dot(h_ref[...], wb_scr[...], preferred_element_type=F32)


def _in_proj(h, w, col0, ncols, tm, tn):
    n, d = h.shape
    off = col0 // tn
    return pl.pallas_call(
        _in_proj_kernel,
        grid=(ncols // tn, n // tm),
        in_specs=[
            pl.BlockSpec((tm, d), lambda j, i: (i, 0)),
            pl.BlockSpec((d, tn), lambda j, i: (0, j + off)),
        ],
        out_specs=pl.BlockSpec((None, tm, tn), lambda j, i: (j, i, 0)),
        out_shape=jax.ShapeDtypeStruct((ncols // tn, n, tn), F32),
        scratch_shapes=[pltpu.VMEM((d, tn), BF16)],
        compiler_params=_params(("arbitrary", "arbitrary")),
        name="in_proj",
    )(h, w)


def _out_proj_kernel(oa_ref, ob_ref, w_ref, x_ref, ga_ref, o_ref):
    m = jnp.dot(oa_ref[...].astype(BF16), w_ref[:D_GROUP, :], preferred_element_type=F32)
    m += jnp.dot(ob_ref[...].astype(BF16), w_ref[D_GROUP:, :], preferred_element_type=F32)
    o_ref[...] = x_ref[...] + ga_ref[...] * m


def _out_proj(oa, ob, w, x, gate, mod_spec2, tm):
    n, d = x.shape
    half = pl.BlockSpec((tm, D_GROUP), lambda i: (i, 0))
    row = pl.BlockSpec((tm, d), lambda i: (i, 0))
    return pl.pallas_call(
        _out_proj_kernel,
        grid=(n // tm,),
        in_specs=[half, half, pl.BlockSpec((2 * D_GROUP, d), lambda i: (0, 0)), row, mod_spec2],
        out_specs=row,
        out_shape=jax.ShapeDtypeStruct((n, d), F32),
        compiler_params=_params(("arbitrary",)),
        name="out_proj",
    )(oa, ob, w, x, gate)


def _gla_kernel(q_ref, f_ref, i_ref, g_ref, lbl_ref, gain_ref, s0_ref, o_ref, s_ref, st_scr,
                *, chunk, n_valid, n_groups):
    t = pl.program_id(1)
    n_chunks = LANE // chunk
    shift = chunk.bit_length() - 1

    @pl.when(t == 0)
    def _():
        for h in range(N_HEADS):
            st_scr[h] = s0_ref[h].T

    r_id = lax.broadcasted_iota(jnp.int32, (LANE, LANE), 0)
    c_id = lax.broadcasted_iota(jnp.int32, (LANE, LANE), 1)
    same = (r_id >> shift) == (c_id >> shift)
    causal = same & (c_id <= r_id)
    m_cum = jnp.concatenate([jnp.where(causal, 1.0, 0.0), jnp.where(same, 1.0, 0.0)],
                            axis=0).astype(BF16)

    heads = [slice(h * HEAD_DIM, (h + 1) * HEAD_DIM) for h in range(N_HEADS)]
    lbl = lbl_ref[...]
    mx = jnp.maximum(lbl, 0.0)
    e1 = jnp.exp(lbl - mx)
    lb = e1 / (e1 + jnp.exp(-mx))

    rows = [slice(g * LANE, (g + 1) * LANE) for g in range(n_groups)]
    kk, bb, v, vb, qd, k_inv, k_end, decay, o, v_t = ([None] * n_groups for _ in range(10))
    st = [st_scr[h] for h in range(N_HEADS)]

    def load(ref, g):
        if n_valid < LANE:
            return jnp.concatenate([ref[...], jnp.zeros((LANE - n_valid, D_GROUP), F32)], axis=0)
        return ref[rows[g], :]

    def gates(g):
        f = lb + (1.0 - lb) * _sigmoid(load(f_ref, g))
        logf = jnp.log(f)
        k = 1.0 - f
        if n_valid < LANE:
            valid = lax.broadcasted_iota(jnp.int32, (LANE, D_GROUP), 0) < n_valid
            logf = jnp.where(valid, logf, 0.0)
            k = jnp.where(valid, k, 0.0)
        hi = logf.astype(BF16)
        r1 = logf - hi.astype(F32)
        mid = r1.astype(BF16)
        lo = (r1 - mid.astype(F32)).astype(BF16)
        kk[g] = k
        bb[g] = jnp.dot(m_cum, jnp.concatenate([hi, mid, lo], axis=1),
                        preferred_element_type=F32)

    def decays(g):
        s = bb[g][:, :D_GROUP] + bb[g][:, D_GROUP:2 * D_GROUP] + bb[g][:, 2 * D_GROUP:]
        b = s[:LANE]
        b_last = s[LANE:]
        v[g] = load(i_ref, g)
        vb[g] = v[g].astype(BF16)
        qd[g] = (load(q_ref, g) * ATT_SCALE * jnp.exp(b)).astype(BF16)
        k_inv[g] = (kk[g] * jnp.exp(-b)).astype(BF16)
        k_end[g] = (kk[g] * jnp.exp(b_last - b)).astype(BF16)
        decay[g] = jnp.exp(b_last)

    def intra(g):
        a = [lax.dot_general(qd[g][:, c], k_inv[g][:, c], NT_DIMS, preferred_element_type=F32)
             for c in heads]
        a = [jnp.where(causal, x, 0.0).astype(BF16) for x in a]
        o[g] = [jnp.dot(x, vb[g][:, c], preferred_element_type=F32) for x, c in zip(a, heads)]
        v_t[g] = [[v[g][j * chunk:(j + 1) * chunk, c].T.astype(BF16) for j in range(n_chunks)]
                  for c in heads]

    def state(g):
        o_state = [[] for _ in heads]
        for j in range(n_chunks):
            for h, c in enumerate(heads):
                o_state[h].append(lax.dot_general(qd[g][j * chunk:(j + 1) * chunk, c],
                                                  st[h].astype(BF16), NT_DIMS,
                                                  preferred_element_type=F32))
                upd = jnp.dot(v_t[g][h][j], k_end[g][j * chunk:(j + 1) * chunk, c],
                              preferred_element_type=F32)
                st[h] = decay[g][j * chunk:j * chunk + 1, c] * st[h] + upd
        for h in range(N_HEADS):
            o[g][h] = o[g][h] + (jnp.concatenate(o_state[h], axis=0) if n_chunks > 1
                                 else o_state[h][0])

    def emit(g):
        gate = _silu(load(g_ref, g))
        for h, c in enumerate(heads):
            out = (_head_norm(o[g][h], gain_ref[:, c]) * gate[:, c]).astype(o_ref.dtype)
            if n_valid < LANE:
                o_ref[:, c] = out[:n_valid]
            else:
                o_ref[rows[g], c] = out

    _skewed(n_groups, [gates, decays, intra, state, emit])
    for h in range(N_HEADS):
        st_scr[h] = st[h]

    @pl.when(t == pl.num_programs(1) - 1)
    def _():
        for h in range(N_HEADS):
            s_ref[h] = st_scr[h].T


def _gla(pa, lb_logits, gain, s0, n_seq, rows, chunk, n_valid, out_dtype):
    n_groups = min(GLA_GROUPS, rows // LANE)
    n_tiles = rows // (n_groups * LANE)
    tile = n_groups * LANE if n_valid == LANE else n_valid
    assert n_valid == LANE or (rows == LANE and n_valid % 8 == 0)

    def col(group):
        return pl.BlockSpec((None, tile, D_GROUP), lambda b, t: (group, b * n_tiles + t, 0))
    vec = pl.BlockSpec((1, D_GROUP), lambda b, t: (0, 0))
    state = pl.BlockSpec((None, N_HEADS, HEAD_DIM, HEAD_DIM), lambda b, t: (b, 0, 0, 0))
    return pl.pallas_call(
        functools.partial(_gla_kernel, chunk=chunk, n_valid=n_valid, n_groups=n_groups),
        grid=(n_seq, n_tiles),
        in_specs=[col(0), col(1), col(2), col(3), vec, vec, state],
        out_specs=(pl.BlockSpec((tile, D_GROUP), lambda b, t: (b * n_tiles + t, 0)), state),
        out_shape=(jax.ShapeDtypeStruct((n_seq * n_tiles * tile, D_GROUP), out_dtype),
                   jax.ShapeDtypeStruct((n_seq, N_HEADS, HEAD_DIM, HEAD_DIM), F32)),
        scratch_shapes=[pltpu.VMEM((N_HEADS, HEAD_DIM, HEAD_DIM), F32)],
        compiler_params=_params(("arbitrary", "arbitrary")),
        name="hgrn2",
    )(pa, pa, pa, pa, lb_logits.reshape(1, D_GROUP), gain.reshape(1, D_GROUP), s0)


def _suffix_weights():
    r_id = lax.broadcasted_iota(jnp.int32, (2 * LANE, 2 * LANE), 0) & (LANE - 1)
    c_id = lax.broadcasted_iota(jnp.int32, (2 * LANE, 2 * LANE), 1)
    return jnp.where((r_id >= c_id) | (c_id >= LANE), 1.0, 0.0).astype(BF16)


def _softplus2(z):
    return jnp.maximum(z, 0.0) + jnp.log(1.0 + jnp.exp2(-jnp.abs(z))) * LOG2E


def _sb_softplus(z_blocks):
    sp = [_softplus2(z) for z in z_blocks]
    hi, lo = _split2(sp[0] if len(sp) == 1 else jnp.concatenate(sp, axis=0))
    return jnp.concatenate([hi, lo], axis=1)


def _sb_weights(z_blocks, rt, run):
    m = z_blocks[0].shape[0]
    out = []
    for n, z in enumerate(z_blocks):
        within = rt[n * m:(n + 1) * m, :LANE]
        total = rt[n * m:(n + 1) * m, LANE:]
        out.append(jnp.exp2(z - within if run is None else z - within - run))
        run = total if run is None else run + total
    return out, run


def _skewed(n_items, stages):
    for tick in range(n_items + len(stages) - 1):
        for s in range(len(stages) - 1, -1, -1):
            i = tick - s
            if 0 <= i < n_items:
                stages[s](i)


def _sbp_kernel(q_ref, k_ref, v_ref, bias_ref, gain_ref, o_ref, q_scr, acc_scr, run_scr):
    i = pl.program_id(1)
    w2 = _suffix_weights()
    q_scr[...] = (q_ref[...] * (ATT_SCALE * LOG2E)).astype(BF16)
    acc_scr[...] = jnp.zeros_like(acc_scr)
    run_scr[...] = jnp.zeros_like(run_scr)
    r_id = lax.broadcasted_iota(jnp.int32, (SB_TILE, LANE), 0)
    c_id = lax.broadcasted_iota(jnp.int32, (SB_TILE, LANE), 1)
    n_blk = SB_TILE // LANE

    def visit(slabs, diagonal):
        n = len(slabs) * N_HEADS
        rows = [pl.ds(pl.multiple_of(j * SB_TILE, SB_TILE), SB_TILE) for j in slabs]
        order = range(n_blk - 1, -1, -1)
        if diagonal:
            mask = [jnp.where(c_id + m * LANE < r_id, 0.0, MASKED) for m in order]
        heads = [slice(h * HEAD_DIM, (h + 1) * HEAD_DIM) for h in range(N_HEADS)]
        zb, hl, rt = [None] * n, [None] * n, [None] * n

        def scores(t):
            r, c = rows[t // N_HEADS], heads[t % N_HEADS]
            z = lax.dot_general(q_scr[:, c], k_ref[r, c].astype(BF16), NT_DIMS,
                                preferred_element_type=F32) + bias_ref[t % N_HEADS]
            zb[t] = [z[:, m * LANE:(m + 1) * LANE] for m in order]
            if diagonal and t < N_HEADS:
                zb[t] = [z_m + m_m for z_m, m_m in zip(zb[t], mask)]

        def softplus(t):
            hl[t] = _sb_softplus(zb[t])

        def suffix(t):
            rt[t] = jnp.dot(hl[t], w2, preferred_element_type=F32)

        def accumulate(t):
            r, h = rows[t // N_HEADS], t % N_HEADS
            a_blocks, total = _sb_weights(zb[t], rt[t], None)
            a = jnp.concatenate(a_blocks[::-1], axis=1).astype(BF16)
            p = jnp.dot(a, v_ref[r, heads[h]].astype(BF16), preferred_element_type=F32)
            run = run_scr[h]
            acc_scr[h] += p * jnp.exp2(-run)
            run_scr[h] = run + total

        _skewed(n, [scores, softplus, suffix, accumulate])

    @pl.when(i == 0)
    def _():
        visit([0], True)

    @pl.when(i > 0)
    def _():
        visit([i, i - 1], True)

        def pair(t, carry):
            visit([i - 2 - 2 * t, i - 3 - 2 * t], False)
            return carry
        lax.fori_loop(0, (i - 1) // 2, pair, 0)

        @pl.when(i % 2 == 0)
        def _():
            visit([0], False)

    for h in range(N_HEADS):
        c = slice(h * HEAD_DIM, (h + 1) * HEAD_DIM)
        o_ref[:, c] = _head_norm(acc_scr[h], gain_ref[:, c]).astype(o_ref.dtype)


def _sb_prompt(pq, q_group, k, v, bias_rows, gain, n_seq, seq):
    nq = seq // SB_TILE
    kv = pl.BlockSpec((seq, D_GROUP), lambda b, i: (b, 0))
    return pl.pallas_call(
        _sbp_kernel,
        grid=(n_seq, nq),
        in_specs=[
            pl.BlockSpec((None, SB_TILE, D_GROUP), lambda b, i: (q_group, b * nq + i, 0)),
            kv, kv,
            pl.BlockSpec((N_HEADS, 1, SB_TILE), lambda b, i: (0, 0, 0)),
            pl.BlockSpec((1, D_GROUP), lambda b, i: (0, 0)),
        ],
        out_specs=pl.BlockSpec((SB_TILE, D_GROUP), lambda b, i: (b * nq + i, 0)),
        out_shape=jax.ShapeDtypeStruct((n_seq * seq, D_GROUP), BF16),
        scratch_shapes=[pltpu.VMEM((SB_TILE, D_GROUP), BF16),
                        pltpu.VMEM((N_HEADS, SB_TILE, HEAD_DIM), F32),
                        pltpu.VMEM((N_HEADS, SB_TILE, LANE), F32)],
        compiler_params=_params(("arbitrary", "arbitrary")),
        name="stickbreak_prompt",
    )(pq, k, v, bias_rows, gain.reshape(1, D_GROUP))


def _sbs_kernel(pt_ref, q_ref, kn_ref, vn_ref, *rest, n_new):
    del pt_ref
    kc_refs = rest[:SB_PAGES]
    vc_refs = rest[SB_PAGES:2 * SB_PAGES]
    bias_ref, gain_ref, o_ref, q_scr, acc_scr, run_scr = rest[2 * SB_PAGES:]
    j = pl.program_id(1)
    nr = N_HEADS * n_new
    w2 = _suffix_weights()
    row = lax.broadcasted_iota(jnp.int32, (nr, LANE), 0)
    col = lax.broadcasted_iota(jnp.int32, (nr, LANE), 1)
    own = (col & (N_HEADS - 1)) == (row >> (n_new.bit_length() - 1))

    def visit(k_rows, v_rows, bias):
        n = len(k_rows)
        q = q_scr[...]
        zb, hl, rt, a = [None] * n, [None] * n, [None] * n, [None] * n
        state = {"run": run_scr[...], "acc": acc_scr[...]}

        def scores(p):
            z = lax.dot_general(q, k_rows[p](), NT_DIMS, preferred_element_type=F32)
            zb[p] = [z[:, m * LANE:(m + 1) * LANE] + bias
                     for m in range(z.shape[1] // LANE - 1, -1, -1)]

        def softplus(p):
            hl[p] = _sb_softplus(zb[p])

        def suffix(p):
            rt[p] = jnp.dot(hl[p], w2, preferred_element_type=F32)

        def weights(p):
            blocks, state["run"] = _sb_weights(zb[p], rt[p], state["run"])
            blocks = blocks[::-1]
            a[p] = (blocks[0] if len(blocks) == 1
                    else jnp.concatenate(blocks, axis=1)).astype(BF16)

        def accumulate(p):
            state["acc"] = state["acc"] + jnp.dot(a[p], v_rows[p](), preferred_element_type=F32)

        _skewed(n, [scores, softplus, suffix, weights, accumulate])
        acc_scr[...] = state["acc"]
        run_scr[...] = state["run"]

    @pl.when(j == 0)
    def _():
        q = q_ref[...] * (ATT_SCALE * LOG2E)
        q_scr[...] = jnp.concatenate(
            [q[:, h * HEAD_DIM:(h + 1) * HEAD_DIM] for h in range(N_HEADS)], axis=0).astype(BF16)
        run_scr[...] = jnp.zeros_like(run_scr)
        acc_scr[...] = jnp.zeros_like(acc_scr)
        pad = jnp.zeros((LANE - nr, HEAD_DIM), F32)
        kb = jnp.concatenate([kn_ref[...], pad], axis=0).astype(BF16)
        vb = jnp.concatenate([vn_ref[...], pad], axis=0).astype(BF16)
        earlier = (col >> (N_HEADS.bit_length() - 1)) < (row & (n_new - 1))
        visit([lambda: kb], [lambda: vb], jnp.where(own & earlier, bias_ref[...], MASKED))

    def page_rows(ref):
        return lambda: ref[...].reshape(PAGE * N_HEADS, HEAD_DIM).astype(BF16)
    visit([page_rows(r) for r in kc_refs], [page_rows(r) for r in vc_refs],
          jnp.where(own, bias_ref[...], MASKED))

    @pl.when(j == pl.num_programs(1) - 1)
    def _():
        acc = acc_scr[...]
        heads = [acc[h * n_new:(h + 1) * n_new] for h in range(N_HEADS)]
        heads = [o * lax.rsqrt(jnp.mean(o * o, axis=-1, keepdims=True) + EPS) for o in heads]
        o_ref[...] = jnp.concatenate(heads, axis=1) * gain_ref[...]


def _sb_sample(pq, q_group, k_new, v_new, cache_k, cache_v, page_table, bias_rep, gain,
               n_seq, n_new):
    n_pages = page_table.shape[1]
    nr = N_HEADS * n_new
    assert nr <= LANE
    new = pl.BlockSpec((None, nr, HEAD_DIM), lambda b, j, pt: (b, 0, 0))

    def page(r):
        return pl.BlockSpec(
            (None, None, PAGE, N_HEADS, HEAD_DIM),
            lambda b, j, pt: (0, pt[b, n_pages - 1 - (j * SB_PAGES + r)], 0, 0, 0))
    pages = [page(r) for r in range(SB_PAGES)]
    grid_spec = pltpu.PrefetchScalarGridSpec(
        num_scalar_prefetch=1,
        grid=(n_seq, n_pages // SB_PAGES),
        in_specs=[pl.BlockSpec((None, n_new, D_GROUP), lambda b, j, pt: (q_group, b, 0)), new, new,
                  *pages, *pages,
                  pl.BlockSpec((nr, LANE), lambda b, j, pt: (0, 0)),
                  pl.BlockSpec((1, D_GROUP), lambda b, j, pt: (0, 0))],
        out_specs=pl.BlockSpec((n_new, D_GROUP), lambda b, j, pt: (b, 0)),
        scratch_shapes=[pltpu.VMEM((nr, HEAD_DIM), BF16), pltpu.VMEM((nr, HEAD_DIM), F32),
                        pltpu.VMEM((nr, LANE), F32)],
    )
    return pl.pallas_call(
        functools.partial(_sbs_kernel, n_new=n_new),
        grid_spec=grid_spec,
        out_shape=jax.ShapeDtypeStruct((n_seq * n_new, D_GROUP), F32),
        compiler_params=_params(("arbitrary", "arbitrary")),
        name="stickbreak_sample",
    )(page_table, pq, k_new, v_new, *([cache_k] * SB_PAGES), *([cache_v] * SB_PAGES),
      bias_rep, gain.reshape(1, D_GROUP))


def kernel(x_prompt, x_sample, cache_k, cache_v, state_hgrn, page_table, c_prompt, c_sample,
           lb_logits, norm_ffn1, norm_mix, norm_ffn2, w_mod, b_mod,
           w_ffn1_gate, w_ffn1_up, w_ffn1_down, w_in, g_out_a, g_out_b, b_sb, w_out,
           w_ffn2_gate, w_ffn2_up, w_ffn2_down, norm_final, w_final_mod, b_final_mod):
    n_p, seq, d = x_prompt.shape
    n_s, n_new, _ = x_sample.shape
    assert w_mod.shape[0] == 1, "single-layer trunk"
    assert n_p + n_s <= MOD_ROWS
    assert n_new & (n_new - 1) == 0 and n_new <= PAGE
    assert page_table.shape[1] % SB_PAGES == 0 and seq % SB_TILE == 0

    c_rows = jnp.concatenate(
        [c_prompt, c_sample, jnp.zeros((MOD_ROWS - n_p - n_s, d), F32)], axis=0)
    mod = _modulation(c_rows, w_mod[0], b_mod[0], PROJ_COLS).reshape(MOD_ROWS, N_MOD, d)
    fmod = _modulation(c_rows, w_final_mod, b_final_mod, PROJ_COLS).reshape(MOD_ROWS, 2, d)

    w_out_b = w_out[0].astype(BF16)
    bias2 = b_sb[0] * LOG2E
    bias_rows = jnp.broadcast_to(bias2.reshape(N_HEADS, 1, 1), (N_HEADS, 1, SB_TILE))
    bias_rep = jnp.broadcast_to(jnp.repeat(bias2, n_new)[:, None], (N_HEADS * n_new, LANE))

    def layer(x, mods, fmods, mod_spec, mod_spec1, mod_spec2, tm, tm2, tf, tm_proj, mixer,
              w_ffn1, w_ffn2, emit):
        x1, h2, *w1 = _ffn(x, norm_ffn1[0], mods[0:3], *w_ffn1, norm_mix[0], mods[3:5],
                           mod_spec, tm, tf, final=False, emit=emit)
        pa = _in_proj(h2, w_in[0], 0, 5 * D_GROUP, tm_proj, PROJ_COLS)
        k_new = _in_proj(h2, w_in[0], 5 * D_GROUP, D_GROUP, tm_proj, PROJ_COLS)[0]
        v_new = _in_proj(h2, w_in[0], 6 * D_GROUP, D_GROUP, tm_proj, PROJ_COLS)[0]
        oa, ob, s_fin = mixer(pa, k_new, v_new)
        x2 = _out_proj(oa, ob, w_out_b, x1, mods[5], mod_spec1, tm)
        y, *w2 = _ffn(x2, norm_ffn2[0], mods[6:9], *w_ffn2, norm_final, fmods,
                      mod_spec2, tm2, tf, final=True, emit=emit)
        return y, k_new, v_new, s_fin, w1, w2

    tiles_per_seq = seq // FFN_ROWS
    mods_p = [mod[:n_p, j].reshape(n_p, 1, d) for j in range(N_MOD)]
    fmods_p = [fmod[:n_p, j].reshape(n_p, 1, d) for j in range(2)]
    spec_p = pl.BlockSpec((None, 1, d), lambda i, f: (i // tiles_per_seq, 0, 0))
    spec_p1 = pl.BlockSpec((None, 1, d), lambda i: (i // tiles_per_seq, 0, 0))
    spec_p2 = pl.BlockSpec((None, 1, d), lambda i, f: (i // (seq // FFN2_ROWS), 0, 0))

    def mixer_p(pa, k_new, v_new):
        s0 = jnp.zeros((n_p, N_HEADS, HEAD_DIM, HEAD_DIM), F32)
        oa, s_fin = _gla(pa, lb_logits[0], g_out_a[0], s0, n_p, seq, GLA_CHUNK, LANE, BF16)
        ob = _sb_prompt(pa, 4, k_new, v_new, bias_rows, g_out_b[0], n_p, seq)
        return oa, ob, s_fin

    rows_s = n_s * n_new
    mods_s = [jnp.repeat(mod[n_p:n_p + n_s, j], n_new, axis=0).reshape(1, rows_s, d)
              for j in range(N_MOD)]
    fmods_s = [jnp.repeat(fmod[n_p:n_p + n_s, j], n_new, axis=0).reshape(1, rows_s, d)
               for j in range(2)]
    spec_s = pl.BlockSpec((None, rows_s, d), lambda i, f: (0, 0, 0))
    spec_s1 = pl.BlockSpec((None, rows_s, d), lambda i: (0, 0, 0))

    def mixer_s(pa, k_new, v_new):
        oa, s_fin = _gla(pa, lb_logits[0], g_out_a[0], state_hgrn[0], n_s, LANE, LANE, n_new, F32)
        rows_th = (n_s, n_new * N_HEADS, HEAD_DIM)
        ob = _sb_sample(pa, 4, k_new.reshape(rows_th), v_new.reshape(rows_th),
                        cache_k, cache_v, page_table, bias_rep, g_out_b[0], n_s, n_new)
        return oa, ob, s_fin

    y_s, k_s, v_s, s_s, w1, w2 = layer(
        x_sample.reshape(rows_s, d), mods_s, fmods_s, spec_s, spec_s1, spec_s, rows_s, rows_s,
        FFN_COLS, rows_s, mixer_s, (w_ffn1_gate[0], w_ffn1_up[0], w_ffn1_down[0]),
        (w_ffn2_gate[0], w_ffn2_up[0], w_ffn2_down[0]), True)
    y_p, k_p, v_p, s_p, _, _ = layer(
        x_prompt.reshape(n_p * seq, d), mods_p, fmods_p, spec_p, spec_p1, spec_p2, FFN_ROWS,
        FFN2_ROWS, FFN_COLS, PROJ_ROWS, mixer_p, w1, w2, False)

    return (y_p.reshape(n_p, seq, d), y_s.reshape(n_s, n_new, d),
            k_p.reshape(1, n_p, seq, N_HEADS, HEAD_DIM), v_p.reshape(1, n_p, seq, N_HEADS, HEAD_DIM),
            k_s.reshape(1, n_s, n_new, N_HEADS, HEAD_DIM), v_s.reshape(1, n_s, n_new, N_HEADS, HEAD_DIM),
            s_p[None], s_s[None])
```
